```python
import math
import jax
import jax.numpy as jnp
from jax import lax
import numpy as np

D_MODEL = 1024
BATCH = 32
SEQ = 2048
DEPTH = 2

HY_WIDTH = 512
HY_SHORT = 3
HY_BANDS = 16
HY_EMB = 1 + 2 * HY_BANDS
HY_FILT = 64
HY_TARGET = 1e-2
HY_FAST_PCT = 0.3
HY_SLOW_PCT = 1.5
S5_WIDTH = 512
S5_GROUP = 16
S5_GROUPS = S5_WIDTH // S5_GROUP
S5_STATE = 64
SSD_WIDTH = 1024
SSD_HEADDIM = 64
SSD_HEADS = SSD_WIDTH // SSD_HEADDIM
SSD_GROUPS = 2
SSD_STATE = 128
SSD_CONV = 5
SSD_CHUNK = 128
FFN_HIDDEN = 2816
FFN_CONV = 3
N_BRANCH = 3
EPS = 1e-6

HY_IN = 3 * HY_WIDTH
SSD_BC = SSD_GROUPS * SSD_STATE
SSD_XBC = SSD_WIDTH + 2 * SSD_BC
GATE_COLS = N_BRANCH * D_MODEL
IN_COLS = HY_IN + S5_WIDTH + SSD_WIDTH + SSD_XBC + 2 * SSD_HEADS + GATE_COLS
IN_SPLITS = (HY_IN,
             HY_IN + S5_WIDTH,
             HY_IN + S5_WIDTH + SSD_WIDTH,
             HY_IN + S5_WIDTH + SSD_WIDTH + SSD_XBC,
             HY_IN + S5_WIDTH + SSD_WIDTH + SSD_XBC + 2 * SSD_HEADS)

kernel_name = 'hybrid_hyena_s5_ssd_encoder'


def rmsnorm(x, w):
    x32 = x.astype(jnp.float32)
    y = x32 * lax.rsqrt(jnp.mean(x32 * x32, axis=-1, keepdims=True) + EPS)
    return y.astype(x.dtype) * w


def dwconv_centred(x, w, b):
    y = lax.conv_general_dilated(x, w[:, None, :], window_strides=(1,), padding='SAME',
                                 dimension_numbers=('NWC', 'WIO', 'NWC'),
                                 feature_group_count=x.shape[-1])
    return y + b


def hyena_filters(L, w1, b1, freq, w2, b2, w3, decay):
    f32 = jnp.float32
    pos = jnp.arange(L, dtype=f32)
    t = (pos / max(L - 1, 1))[:, None]
    bands = jnp.linspace(1e-4, HY_BANDS - 1, HY_BANDS, dtype=f32)
    ang = (2.0 * math.pi / L) * pos[:, None] * bands[None, :]
    z = jnp.concatenate([t, jnp.cos(ang), -jnp.sin(ang)], axis=-1)
    h = jnp.sin(freq[0] * (z @ w1 + b1))
    h = jnp.sin(freq[1] * (h @ w2 + b2))
    h = (h @ w3) * jnp.exp(-t * jnp.abs(decay))
    h = h.astype(f32).reshape(L, 2, HY_WIDTH)
    k = jnp.concatenate([h[:, 0], jnp.zeros((1, HY_WIDTH), f32), h[:0:-1, 1]], axis=0)
    return k / jnp.sum(jnp.abs(k), axis=0, keepdims=True)


def hyena_mixer(u, short_w, short_b, w1, b1, freq, w2, b2, w3, decay, bias):
    L = u.shape[1]
    uc = dwconv_centred(u, short_w, short_b)
    x0, x1, v = jnp.split(uc, 3, axis=-1)
    s = v * x1
    k = hyena_filters(L, w1, b1, freq, w2, b2, w3, decay)
    S = jnp.fft.rfft(s.astype(jnp.float32), n=2 * L, axis=1)
    K = jnp.fft.rfft(k, n=2 * L, axis=0)
    y = jnp.fft.irfft(S * K[None], n=2 * L, axis=1)[:, :L].astype(u.dtype)
    return x0 * (y + bias * s)


def _cmul(ar, ai, br, bi):
    return ar * br - ai * bi, ar * bi + ai * br


def _s5_combine(e1, e2):
    a1r, a1i, b1r, b1i = e1
    a2r, a2i, b2r, b2i = e2
    ar, ai = _cmul(a2r, a2i, a1r, a1i)
    br, bi = _cmul(a2r, a2i, b1r, b1i)
    return ar, ai, br + b2r, bi + b2i


def s5_direction(ug, a_re, a_im, log_dt, b_re, b_im, c_re, c_im, reverse):
    f32 = jnp.float32
    a_re = a_re.astype(f32)
    a_im = a_im.astype(f32)
    dt = jnp.exp(log_dt.astype(f32))[:, None]
    mag = jnp.exp(a_re * dt)
    abar_r, abar_i = mag * jnp.cos(a_im * dt), mag * jnp.sin(a_im * dt)
    den = a_re * a_re + a_im * a_im
    nr = abar_r - 1.0
    f_r = (nr * a_re + abar_i * a_im) / den
    f_i = (abar_i * a_re - nr * a_im) / den
    bu_r = jnp.einsum('blgc,gnc->lbgn', ug, b_re.astype(f32))
    bu_i = jnp.einsum('blgc,gnc->lbgn', ug, b_im.astype(f32))
    b_r, b_i = _cmul(f_r, f_i, bu_r, bu_i)
    shape = (ug.shape[1], 1) + abar_r.shape
    _, _, h_r, h_i = lax.associative_scan(
        _s5_combine,
        (jnp.broadcast_to(abar_r, shape), jnp.broadcast_to(abar_i, shape), b_r, b_i),
        reverse=reverse, axis=0)
    return (jnp.einsum('lbgn,gcn->blgc', h_r, c_re.astype(f32))
            - jnp.einsum('lbgn,gcn->blgc', h_i, c_im.astype(f32)))


def s5_mixer(u, a_re, a_im, log_dt, b_re, b_im, c_re, c_im, d, w_glu, b_glu):
    Bsz, L, _ = u.shape
    ug = u.astype(jnp.float32).reshape(Bsz, L, S5_GROUPS, S5_GROUP)
    y = (s5_direction(ug, a_re[0], a_im[0], log_dt[0], b_re[0], b_im[0], c_re[0], c_im[0], False)
         + s5_direction(ug, a_re[1], a_im[1], log_dt[1], b_re[1], b_im[1], c_re[1], c_im[1], True))
    y = y.reshape(Bsz, L, S5_WIDTH).astype(u.dtype) + d * u
    g = jax.nn.gelu(y)
    return g * jax.nn.sigmoid(g @ w_glu + b_glu)


def _segsum_exp(a_cs):
    Q = a_cs.shape[-1]
    diff = a_cs[..., :, None] - a_cs[..., None, :]
    mask = jnp.tril(jnp.ones((Q, Q), dtype=bool))
    return jnp.where(mask, jnp.exp(jnp.where(mask, diff, 0.0)), 0.0)


def ssd_scan(x, dt, A, Bm, Cm):
    Bsz, L, H, P = x.shape
    G, N, Q = SSD_GROUPS, SSD_STATE, SSD_CHUNK
    Hg, nc = H // G, L // Q
    xd = (x * dt[..., None]).reshape(Bsz, nc, Q, G, Hg, P)
    a = jnp.moveaxis((dt * A).reshape(Bsz, nc, Q, G, Hg), 2, -1)
    a_cs = jnp.cumsum(a, axis=-1)
    Bc = Bm.reshape(Bsz, nc, Q, G, N)
    Cc = Cm.reshape(Bsz, nc, Q, G, N)
    scores = jnp.einsum('bclgn,bcsgn->bcgls', Cc, Bc)
    M = scores[:, :, :, None] * _segsum_exp(a_cs)
    y_diag = jnp.einsum('bcghls,bcsghp->bclghp', M, xd)
    decay_states = jnp.moveaxis(jnp.exp(a_cs[..., -1:] - a_cs), -1, 2)
    states = jnp.einsum('bcsgn,bcsghp->bcghpn', Bc, xd * decay_states[..., None])
    chunk_decay = jnp.exp(a_cs[..., -1])

    def step(carry, inp):
        st, dec = inp
        return carry * dec[..., None, None] + st, carry

    init = jnp.zeros((Bsz, G, Hg, P, N), x.dtype)
    _, prev = lax.scan(step, init, (jnp.moveaxis(states, 1, 0), jnp.moveaxis(chunk_decay, 1, 0)))
    prev = jnp.moveaxis(prev, 0, 1)
    in_decay = jnp.moveaxis(jnp.exp(a_cs), -1, 2)[..., None]
    y_off = jnp.einsum('bclgn,bcghpn->bclghp', Cc, prev) * in_decay
    return (y_diag + y_off).reshape(Bsz, L, H, P)


def ssd_mixer(z, xbc, dt_raw, conv_w, conv_b, a_log, dt_bias, d, norm_w):
    f32 = jnp.float32
    Bsz, L, _ = z.shape
    xbc = jax.nn.silu(dwconv_centred(xbc, conv_w, conv_b))
    xs, Bm, Cm = jnp.split(xbc, (SSD_WIDTH, SSD_WIDTH + SSD_BC), axis=-1)
    x32 = xs.astype(f32).reshape(Bsz, L, SSD_HEADS, SSD_HEADDIM)
    B32 = Bm.astype(f32).reshape(Bsz, L, SSD_GROUPS, SSD_STATE)
    C32 = Cm.astype(f32).reshape(Bsz, L, SSD_GROUPS, SSD_STATE)
    dt = jax.nn.softplus(dt_raw.astype(f32).reshape(Bsz, L, 2, SSD_HEADS) + dt_bias.astype(f32))
    A = -jnp.exp(a_log.astype(f32))
    flip = lambda t: jnp.flip(t, axis=1)
    y_f = ssd_scan(x32, dt[:, :, 0], A[0], B32, C32)
    y_b = flip(ssd_scan(flip(x32), flip(dt[:, :, 1]), A[1], flip(B32), flip(C32)))
    y = y_f + y_b + d.astype(f32)[:, None] * x32
    y = y.reshape(Bsz, L, SSD_WIDTH) * jax.nn.silu(z.astype(f32))
    return rmsnorm(y, norm_w).astype(z.dtype)


def conv_ffn(h, w_up, conv_w, conv_b, w_down):
    u = dwconv_centred(h @ w_up, conv_w, conv_b)
    gate, val = jnp.split(u, 2, axis=-1)
    return (jax.nn.silu(gate) * val) @ w_down


def setup_inputs(seed: int = 0) -> dict:
    key = jax.random.key(seed)
    ks = iter(jax.random.split(key, 64))
    f32 = jnp.float32

    def nrm(shape, scale):
        return scale * jax.random.normal(next(ks), shape, f32)

    def gain(shape):
        return 1.0 + nrm(shape, 0.01)

    Ld = DEPTH
    hy_lo = -math.log(HY_TARGET) / HY_SLOW_PCT
    hy_hi = -math.log(HY_TARGET) / HY_FAST_PCT
    hy_decay = jnp.linspace(hy_lo, hy_hi, 2 * HY_WIDTH, dtype=f32) * (1.0 + nrm((Ld, 2 * HY_WIDTH), 0.02))
    s5_a_re = -0.5 + nrm((Ld, 2, S5_GROUPS, S5_STATE), 0.01)
    s5_a_im = math.pi * jnp.arange(S5_STATE, dtype=f32) + nrm((Ld, 2, S5_GROUPS, S5_STATE), 0.01)
    s5_log_dt = jax.random.uniform(next(ks), (Ld, 2, S5_GROUPS), f32, math.log(1e-3), math.log(1e-1))
    ssd_dt0 = jnp.exp(jax.random.uniform(next(ks), (Ld, 2, SSD_HEADS), f32, math.log(1e-3), math.log(1e-1)))
    ssd_dt_bias = ssd_dt0 + jnp.log(-jnp.expm1(-ssd_dt0))
    ssd_a_log = jnp.log(jax.random.uniform(next(ks), (Ld, 2, SSD_HEADS), f32, 1.0, 16.0))
    return {
        'x': nrm((BATCH, SEQ, D_MODEL), 1.0),
        'norm_mix': gain((Ld, D_MODEL)),
        'w_in': nrm((Ld, D_MODEL, IN_COLS), D_MODEL ** -0.5),
        'hy_short_w': nrm((Ld, HY_SHORT, HY_IN), HY_SHORT ** -0.5),
        'hy_short_b': nrm((Ld, HY_IN), 0.01),
        'hy_w1': nrm((Ld, HY_EMB, HY_FILT), HY_EMB ** -0.5),
        'hy_b1': nrm((Ld, HY_FILT), 0.1),
        'hy_freq': gain((Ld, 2, HY_FILT)),
        'hy_w2': nrm((Ld, HY_FILT, HY_FILT), HY_FILT ** -0.5),
        'hy_b2': nrm((Ld, HY_FILT), 0.1),
        'hy_w3': nrm((Ld, HY_FILT, 2 * HY_WIDTH), HY_FILT ** -0.5),
        'hy_decay': hy_decay,
        'hy_bias': nrm((Ld, HY_WIDTH), 1.0),
        's5_a_re': s5_a_re,
        's5_a_im': s5_a_im,
        's5_log_dt': s5_log_dt,
        's5_b_re': nrm((Ld, 2, S5_GROUPS, S5_STATE, S5_GROUP), (2 * S5_GROUP) ** -0.5),
        's5_b_im': nrm((Ld, 2, S5_GROUPS, S5_STATE, S5_GROUP), (2 * S5_GROUP) ** -0.5),
        's5_c_re': nrm((Ld, 2, S5_GROUPS, S5_GROUP, S5_STATE), S5_STATE ** -0.5),
        's5_c_im': nrm((Ld, 2, S5_GROUPS, S5_GROUP, S5_STATE), S5_STATE ** -0.5),
        's5_d': gain((Ld, S5_WIDTH)),
        's5_w_glu': nrm((Ld, S5_WIDTH, S5_WIDTH), S5_WIDTH ** -0.5),
        's5_b_glu': nrm((Ld, S5_WIDTH), 0.01),
        'ssd_conv_w': nrm((Ld, SSD_CONV, SSD_XBC), SSD_CONV ** -0.5),
        'ssd_conv_b': nrm((Ld, SSD_XBC), 0.01),
        'ssd_a_log': ssd_a_log,
        'ssd_dt_bias': ssd_dt_bias,
        'ssd_d': gain((Ld, SSD_HEADS)),
        'ssd_norm': gain((Ld, SSD_WIDTH)),
        'p_a': nrm((Ld, HY_WIDTH, D_MODEL), HY_WIDTH ** -0.5),
        'p_b': nrm((Ld, S5_WIDTH, D_MODEL), S5_WIDTH ** -0.5),
        'p_c': nrm((Ld, SSD_WIDTH, D_MODEL), SSD_WIDTH ** -0.5),
        'w_out': nrm((Ld, D_MODEL, D_MODEL), D_MODEL ** -0.5),
        'norm_ffn': gain((Ld, D_MODEL)),
        'ffn_up': nrm((Ld, D_MODEL, 2 * FFN_HIDDEN), D_MODEL ** -0.5),
        'ffn_conv_w': nrm((Ld, FFN_CONV, 2 * FFN_HIDDEN), FFN_CONV ** -0.5),
        'ffn_conv_b': nrm((Ld, 2 * FFN_HIDDEN), 0.01),
        'ffn_down': nrm((Ld, FFN_HIDDEN, D_MODEL), FFN_HIDDEN ** -0.5),
        'norm_final': gain((D_MODEL,)),
    }


def reference(x, norm_mix, w_in, hy_short_w, hy_short_b, hy_w1, hy_b1, hy_freq, hy_w2, hy_b2, hy_w3,
              hy_decay, hy_bias, s5_a_re, s5_a_im, s5_log_dt, s5_b_re, s5_b_im, s5_c_re, s5_c_im, s5_d,
              s5_w_glu, s5_b_glu, ssd_conv_w, ssd_conv_b, ssd_a_log, ssd_dt_bias, ssd_d, ssd_norm,
              p_a, p_b, p_c, w_out, norm_ffn, ffn_up, ffn_conv_w, ffn_conv_b, ffn_down, norm_final):
    for l in range(DEPTH):
        h = rmsnorm(x, norm_mix[l])
        proj = h @ w_in[l]
        hy_u, s5_u, ssd_z, ssd_xbc, ssd_dt, gate_logits = jnp.split(proj, IN_SPLITS, axis=-1)
        y_a = hyena_mixer(hy_u, hy_short_w[l], hy_short_b[l], hy_w1[l], hy_b1[l], hy_freq[l],
                          hy_w2[l], hy_b2[l], hy_w3[l], hy_decay[l], hy_bias[l])
        y_b = s5_mixer(s5_u, s5_a_re[l], s5_a_im[l], s5_log_dt[l], s5_b_re[l], s5_b_im[l],
                       s5_c_re[l], s5_c_im[l], s5_d[l], s5_w_glu[l], s5_b_glu[l])
        y_c = ssd_mixer(ssd_z, ssd_xbc, ssd_dt, ssd_conv_w[l], ssd_conv_b[l], ssd_a_log[l],
                        ssd_dt_bias[l], ssd_d[l], ssd_norm[l])
        g_a, g_b, g_c = jnp.split(jax.nn.sigmoid(gate_logits), N_BRANCH, axis=-1)
        merged = g_a * (y_a @ p_a[l]) + g_b * (y_b @ p_b[l]) + g_c * (y_c @ p_c[l])
        x = x + merged @ w_out[l]
        h = rmsnorm(x, norm_ffn[l])
        x = x + conv_ffn(h, ffn_up[l], ffn_conv_w[l], ffn_conv_b[l], ffn_down[l])
    return rmsnorm(x, norm_final)
```

```python
import functools
import math

import jax
import jax.numpy as jnp
from jax import lax
from jax.experimental import pallas as pl
from jax.experimental.pallas import tpu as pltpu

F32 = jnp.float32
BF16 = jnp.bfloat16

D_MODEL = 1024
SEQ = 2048
DEPTH = 2
EPS = 1e-6

HY_WIDTH = 512
HY_BANDS = 16
NFFT = 2 * SEQ

S5_WIDTH = 512
S5_GROUP = 16
S5_GROUPS = S5_WIDTH // S5_GROUP
S5_STATE = 64
S5_CHUNK = 16
S5_NCHUNK = SEQ // S5_CHUNK
S5_SLABS = S5_WIDTH // 128

SSD_WIDTH = 1024
SSD_HEADDIM = 64
SSD_HEADS = SSD_WIDTH // SSD_HEADDIM
SSD_GROUPS = 2
SSD_STATE = 128
SSD_BC = SSD_GROUPS * SSD_STATE
SSD_XBC = SSD_WIDTH + 2 * SSD_BC
SSD_CONV = 5
SSD_CHUNK = 128
SSD_NCHUNK = SEQ // SSD_CHUNK
HEADS_PER_GROUP = SSD_HEADS // SSD_GROUPS

FFN_HIDDEN = 2816
FFN_TILE = 256
FFN_NTILE = FFN_HIDDEN // FFN_TILE

PROJ_TM = 512
HY_TILE = 256
VMEM_LIMIT = 56 * 1024 * 1024


def _cparams(sem):
    return pltpu.CompilerParams(dimension_semantics=sem, vmem_limit_bytes=VMEM_LIMIT)


def _const_spec(shape):
    nd = len(shape)
    return pl.BlockSpec(shape, lambda *_: (0,) * nd, pipeline_mode=pl.Buffered(1))


def _rms(x, w):
    return x * lax.rsqrt(jnp.mean(x * x, axis=-1, keepdims=True) + EPS) * w


def _dot(a, b):
    return jnp.dot(a, b, preferred_element_type=F32)


def _dot_nt(a, b):
    return lax.dot_general(a, b, (((1,), (1,)), ((), ())), preferred_element_type=F32)


def _dot_tn(a, b):
    return lax.dot_general(a, b, (((0,), (0,)), ((), ())), preferred_element_type=F32)


def _split3(a):
    a1 = a.astype(BF16)
    r1 = a - a1.astype(F32)
    a2 = r1.astype(BF16)
    a3 = (r1 - a2.astype(F32)).astype(BF16)
    return a1, a2, a3


def _shift_rows(x, j, row):
    n = x.shape[0]
    rolled = pltpu.roll(x, (-j) % n, 0)
    valid = (row >= -j) if j < 0 else (row < n - j)
    return jnp.where(valid, rolled, 0.0)


def _dft_kernel(fc_ref, fs_ref):
    rows = fc_ref.shape[0]
    r = lax.broadcasted_iota(jnp.int32, (rows, SEQ), 0) + pl.program_id(0) * rows
    c = lax.broadcasted_iota(jnp.int32, (rows, SEQ), 1)
    ang = ((r * c) & (NFFT - 1)).astype(F32) * (2.0 * math.pi / NFFT)
    fc_ref[...] = jnp.cos(ang).astype(BF16)
    fs_ref[...] = jnp.sin(ang).astype(BF16)


def _dft_matrices():
    rows = 256
    return pl.pallas_call(
        _dft_kernel,
        grid=(SEQ // rows,),
        out_specs=[pl.BlockSpec((rows, SEQ), lambda i: (i, 0))] * 2,
        out_shape=[jax.ShapeDtypeStruct((SEQ, SEQ), BF16)] * 2,
        compiler_params=_cparams(("arbitrary",)),
        name="dft_matrices",
    )()


def _proj_kernel(x_ref, nw_ref, why_ref, ws5_ref, wz_ref, wxbc_ref, wdt_ref, wdtt_ref, wg_ref,
                 hy_ref, s5_ref, z_ref, xbc_ref, dt_ref, dtt_ref, g_ref):
    h = _rms(x_ref[...], nw_ref[...]).astype(BF16)
    hy_ref[...] = _dot(h, why_ref[...]).astype(BF16)
    u = _dot(h, ws5_ref[...])
    for j in range(S5_SLABS):
        s5_ref[j] = u[:, 128 * j:128 * (j + 1)]
    z_ref[...] = _dot(h, wz_ref[...]).astype(BF16)
    xbc_ref[...] = _dot(h, wxbc_ref[...]).astype(BF16)
    dt_ref[...] = _dot(h, wdt_ref[...])
    dtt_ref[...] = _dot_nt(wdtt_ref[...], h)
    g_ref[...] = _dot(h, wg_ref[...]).astype(BF16)


def _proj(x2, nw, w):
    t = x2.shape[0]
    tm = PROJ_TM
    row = lambda n: pl.BlockSpec((tm, n), lambda i: (i, 0))
    return pl.pallas_call(
        _proj_kernel,
        grid=(t // tm,),
        in_specs=[row(D_MODEL), _const_spec((1, D_MODEL)),
                  _const_spec(w["hy"].shape), _const_spec(w["s5"].shape), _const_spec(w["z"].shape),
                  _const_spec(w["xbc"].shape), _const_spec(w["dt"].shape), _const_spec(w["dtt"].shape),
                  _const_spec(w["gate"].shape)],
        out_specs=[row(3 * HY_WIDTH),
                   pl.BlockSpec((S5_SLABS, tm, 128), lambda i: (0, i, 0)),
                   row(SSD_WIDTH), row(SSD_XBC), row(2 * SSD_HEADS),
                   pl.BlockSpec((2 * SSD_HEADS, tm), lambda i: (0, i)),
                   row(3 * D_MODEL)],
        out_shape=[jax.ShapeDtypeStruct((t, 3 * HY_WIDTH), BF16),
                   jax.ShapeDtypeStruct((S5_SLABS, t, 128), F32),
                   jax.ShapeDtypeStruct((t, SSD_WIDTH), BF16),
                   jax.ShapeDtypeStruct((t, SSD_XBC), BF16),
                   jax.ShapeDtypeStruct((t, 2 * SSD_HEADS), F32),
                   jax.ShapeDtypeStruct((2 * SSD_HEADS, t), F32),
                   jax.ShapeDtypeStruct((t, 3 * D_MODEL), BF16)],
        compiler_params=_cparams(("arbitrary",)),
        name="in_proj",
    )(x2, nw, w["hy"], w["s5"], w["z"], w["xbc"], w["dt"], w["dtt"], w["gate"])


def _hyena_kernel(x0_ref, x1_ref, v_ref, sw_ref, sb_ref, fc_ref, fs_ref, kre_ref, kim_ref, knyq_ref,
                  bias_ref, o_ref):
    row = lax.broadcasted_iota(jnp.int32, (SEQ, HY_TILE), 0)

    def short_conv(ref, p):
        u = ref[...].astype(F32)
        w = sw_ref[p]
        return (sb_ref[p] + w[0:1] * _shift_rows(u, -1, row) + w[1:2] * u
                + w[2:3] * _shift_rows(u, 1, row))

    x1 = short_conv(x1_ref, 1)
    v = short_conv(v_ref, 2)
    s = v * x1
    sb = s.astype(BF16)
    sre = _dot(fc_ref[...], sb)
    sim = _dot(fs_ref[...], sb)
    kre = kre_ref[...]
    kim = kim_ref[...]
    yre = (sre * kre + sim * kim).astype(BF16)
    yim = (sim * kre - sre * kim).astype(BF16)
    y = _dot(fc_ref[...], yre) + _dot(fs_ref[...], yim)
    alt = jnp.where((row & 1) == 0, 1.0, -1.0)
    snyq = jnp.sum(s * alt, axis=0, keepdims=True)
    y = y + alt * (snyq * knyq_ref[...])
    x0 = short_conv(x0_ref, 0)
    o_ref[...] = (x0 * (y + bias_ref[...] * s)).astype(o_ref.dtype)


def _hyena(hy_u, hw, fc, fs):
    t = hy_u.shape[0]
    nb = t // SEQ
    nj = HY_WIDTH // HY_TILE
    part = lambda p: pl.BlockSpec((SEQ, HY_TILE), lambda j, b, p=p: (b, p * nj + j))
    ctile = lambda r: pl.BlockSpec((r, HY_TILE), lambda j, b: (0, j))
    return pl.pallas_call(
        _hyena_kernel,
        grid=(nj, nb),
        in_specs=[part(0), part(1), part(2),
                  pl.BlockSpec((3, 3, HY_TILE), lambda j, b: (0, 0, j)),
                  pl.BlockSpec((3, 1, HY_TILE), lambda j, b: (0, 0, j)),
                  _const_spec((SEQ, SEQ)), _const_spec((SEQ, SEQ)),
                  ctile(SEQ), ctile(SEQ), ctile(1), ctile(1)],
        out_specs=pl.BlockSpec((SEQ, HY_TILE), lambda j, b: (b, j)),
        out_shape=jax.ShapeDtypeStruct((t, HY_WIDTH), BF16),
        compiler_params=_cparams(("arbitrary", "arbitrary")),
        name="hyena",
    )(hy_u, hy_u, hy_u, hw["sw"], hw["sb"], fc, fs, hw["kre"], hw["kim"], hw["knyq"], hw["bias"])


def _s5_kernel(u_ref, tb_ref, cg_ref, sc_ref, dcol_ref, wgt_ref, bcol_ref, o_ref, ut_scr, yt_scr):
    nk = S5_NCHUNK
    for t in range(S5_CHUNK):
        for j in range(S5_SLABS):
            blk = u_ref[j, pl.ds(t, nk, stride=S5_CHUNK), :]
            ut_scr[t, 128 * j:128 * (j + 1), :] = blk.T

    half = S5_STATE
    lane = lax.broadcasted_iota(jnp.int32, (half, nk), 1)

    def from_prev(x, s):
        return jnp.where(lane >= s, pltpu.roll(x, s, 1), 0.0)

    def from_next(x, s):
        return jnp.where(lane < nk - s, pltpu.roll(x, nk - s, 1), 0.0)

    def group(g, carry):
        r0 = pl.multiple_of(g * S5_GROUP, S5_GROUP)
        ucol = ut_scr[:, pl.ds(r0, S5_GROUP), :].reshape(S5_CHUNK * S5_GROUP, nk).astype(BF16)
        r = _dot(tb_ref[g], ucol)
        z = r[0:256]
        hfr, hbr = r[256:256 + half], r[256 + half:256 + 2 * half]
        hfi, hbi = r[256 + 2 * half:256 + 3 * half], r[256 + 3 * half:512]
        sc = sc_ref[g]
        for lvl in range(7):
            s = 1 << lvl
            ar = sc[:, 2 * lvl:2 * lvl + 1]
            ai = sc[:, 2 * lvl + 1:2 * lvl + 2]
            arf, arb, aif, aib = ar[:half], ar[half:], ai[:half], ai[half:]
            pfr, pfi = from_prev(hfr, s), from_prev(hfi, s)
            pbr, pbi = from_next(hbr, s), from_next(hbi, s)
            hfr, hfi = hfr + arf * pfr - aif * pfi, hfi + arf * pfi + aif * pfr
            hbr, hbi = hbr + arb * pbr - aib * pbi, hbi + arb * pbi + aib * pbr
        hs = jnp.concatenate([from_prev(hfr, 1), from_next(hbr, 1), from_prev(hfi, 1), from_next(hbi, 1)],
                             axis=0).astype(BF16)
        z = z + _dot(cg_ref[g], hs)
        yt_scr[:, pl.ds(r0, S5_GROUP), :] = z.reshape(S5_CHUNK, S5_GROUP, nk)
        return carry

    lax.fori_loop(0, S5_GROUPS, group, 0)

    for t in range(S5_CHUNK):
        gl = jax.nn.gelu(yt_scr[t] + dcol_ref[...] * ut_scr[t])
        z2 = _dot(wgt_ref[...], gl.astype(BF16)) + bcol_ref[...]
        yb = gl * jax.nn.sigmoid(z2)
        for j in range(S5_SLABS):
            o_ref[j, pl.ds(t, nk, stride=S5_CHUNK), :] = yb[128 * j:128 * (j + 1), :].T


def _s5(s5_u, sw):
    t = s5_u.shape[1]
    nb = t // SEQ
    slab = pl.BlockSpec((S5_SLABS, SEQ, 128), lambda b: (0, b, 0))
    return pl.pallas_call(
        _s5_kernel,
        grid=(nb,),
        in_specs=[slab, _const_spec(sw["tb"].shape), _const_spec(sw["cg"].shape), _const_spec(sw["sc"].shape),
                  _const_spec(sw["dcol"].shape), _const_spec(sw["wgt"].shape), _const_spec(sw["bcol"].shape)],
        out_specs=slab,
        out_shape=jax.ShapeDtypeStruct((S5_SLABS, t, 128), F32),
        scratch_shapes=[pltpu.VMEM((S5_CHUNK, S5_WIDTH, S5_NCHUNK), F32),
                        pltpu.VMEM((S5_CHUNK, S5_WIDTH, S5_NCHUNK), F32)],
        compiler_params=_cparams(("arbitrary",)),
        name="s5",
    )(s5_u, sw["tb"], sw["cg"], sw["sc"], sw["dcol"], sw["wgt"], sw["bcol"])


def _ssd_kernel(xbc_ref, dt_ref, dtt_ref, cw_ref, cb_ref, a_ref, at_ref, db_ref, dbt_ref, dexp_ref,
                ef_ref, eb_ref, tril_ref, triu_ref, o_ref,
                xs_scr, bc_scr, y_scr, carry_scr, dt_scr, a_scr, dtt_scr, att_scr):
    q = SSD_CHUNK
    nh = SSD_HEADS
    cblk = 128
    row = lax.broadcasted_iota(jnp.int32, (SEQ, cblk), 0)

    for i in range(SSD_XBC // cblk):
        c0 = cblk * i
        xb = xbc_ref[:, c0:c0 + cblk].astype(F32)
        w = cw_ref[:, c0:c0 + cblk]
        acc = cb_ref[:, c0:c0 + cblk] + w[2:3] * xb
        for k in (0, 1, 3, 4):
            acc = acc + w[k:k + 1] * _shift_rows(xb, k - SSD_CONV // 2, row)
        act = acc * jax.nn.sigmoid(acc)
        if c0 < SSD_WIDTH:
            xs_scr[:, c0:c0 + cblk] = act.astype(BF16)
        else:
            bc_scr[:, c0 - SSD_WIDTH:c0 - SSD_WIDTH + cblk] = act.astype(BF16)

    dtv = jax.nn.softplus(dt_ref[...] + db_ref[...])
    dt_scr[...] = dtv
    a_scr[...] = dtv * a_ref[...]
    dtvt = jax.nn.softplus(dtt_ref[...] + dbt_ref[...])
    avt = dtvt * at_ref[...]
    for c in range(SSD_NCHUNK):
        dtt_scr[c] = dtvt[:, q * c:q * (c + 1)]
        att_scr[c] = avt[:, q * c:q * (c + 1)]

    ri = lax.broadcasted_iota(jnp.int32, (q, q), 0)
    ci = lax.broadcasted_iota(jnp.int32, (q, q), 1)
    causal = ri >= ci
    anti = ci >= ri
    col_is_f = lax.broadcasted_iota(jnp.int32, (q, 2 * nh), 1) < nh
    row_is_f = lax.broadcasted_iota(jnp.int32, (2 * nh, q), 0) < nh
    lane_lo = lax.broadcasted_iota(jnp.int32, (q, 128), 1) < SSD_HEADDIM
    tril = tril_ref[...]
    triu = triu_ref[...]

    def col_cumsum(a_c):
        parts = _split3(a_c)
        lo = sum(_dot(tril, p) for p in parts)
        hi = sum(_dot(triu, p) for p in parts)
        return jnp.where(col_is_f, lo, hi)

    def row_cumsum(at_c):
        parts = _split3(at_c)
        lo = sum(_dot(p, triu) for p in parts)
        hi = sum(_dot(p, tril) for p in parts)
        return jnp.where(row_is_f, lo, hi)

    def expand(v, e_ref):
        v1 = v.astype(BF16)
        v2 = (v - v1.astype(F32)).astype(BF16)
        return _dot(v1, e_ref[...]) + _dot(v2, e_ref[...])

    def inter_chunk(c, e_ref, edge):
        r0 = pl.multiple_of(c * q, q)
        cs = col_cumsum(a_scr[pl.ds(r0, q), :])
        total = jnp.where(col_is_f[0:1], cs[q - 1:q], cs[0:1])
        w_state = expand(dt_scr[pl.ds(r0, q), :] * jnp.exp(total - cs), e_ref)
        w_in = expand(jnp.exp(cs), e_ref)
        xdd = (xs_scr[pl.ds(r0, q), :] * w_state).astype(BF16)
        outs = []
        for g in range(SSD_GROUPS):
            bg = bc_scr[pl.ds(r0, q), SSD_STATE * g:SSD_STATE * (g + 1)]
            cg = bc_scr[pl.ds(r0, q), SSD_BC + SSD_STATE * g:SSD_BC + SSD_STATE * (g + 1)]
            lanes = slice(512 * g, 512 * (g + 1))
            prev = carry_scr[g]
            outs.append(_dot(cg, prev.astype(BF16)) * w_in[:, lanes])
            carry_scr[g] = prev * w_in[edge:edge + 1, lanes] + _dot_tn(bg, xdd[:, lanes])
        return r0, jnp.concatenate(outs, axis=1)

    def forward_chunk(c, carry):
        r0 = pl.multiple_of(c * q, q)
        cs = col_cumsum(a_scr[pl.ds(r0, q), :])
        rs = row_cumsum(att_scr[c])
        dtt_c = dtt_scr[c]
        xs_b = xs_scr[pl.ds(r0, q), :]
        pairs = []
        for g in range(SSD_GROUPS):
            bg = bc_scr[pl.ds(r0, q), SSD_STATE * g:SSD_STATE * (g + 1)]
            cg = bc_scr[pl.ds(r0, q), SSD_BC + SSD_STATE * g:SSD_BC + SSD_STATE * (g + 1)]
            scores = _dot_nt(cg, bg)
            for j in range(HEADS_PER_GROUP // 2):
                ms = []
                for h in (HEADS_PER_GROUP * g + 2 * j, HEADS_PER_GROUP * g + 2 * j + 1):
                    hb = nh + h
                    lf = jnp.where(causal, jnp.exp(cs[:, h:h + 1] - rs[h:h + 1, :]), 0.0) * dtt_c[h:h + 1, :]
                    lb = jnp.where(anti, jnp.exp(cs[:, hb:hb + 1] - rs[hb:hb + 1, :]), 0.0) * dtt_c[hb:hb + 1, :]
                    ms.append((scores * (lf + lb)).astype(BF16))
                blk = HEADS_PER_GROUP // 2 * g + j
                xp = xs_b[:, 128 * blk:128 * (blk + 1)]
                rhs = jnp.concatenate([jnp.where(lane_lo, xp, 0), jnp.where(lane_lo, 0, xp)], axis=0)
                pairs.append(_dot(jnp.concatenate(ms, axis=1), rhs))
        _, y_off = inter_chunk(c, ef_ref, q - 1)
        y_scr[pl.ds(r0, q), :] = jnp.concatenate(pairs, axis=1) + y_off
        return carry

    def backward_chunk(i, carry):
        r0, y_off = inter_chunk(SSD_NCHUNK - 1 - i, eb_ref, 0)
        y_scr[pl.ds(r0, q), :] += y_off
        return carry

    carry_scr[...] = jnp.zeros_like(carry_scr)
    lax.fori_loop(0, SSD_NCHUNK, forward_chunk, 0)
    carry_scr[...] = jnp.zeros_like(carry_scr)
    lax.fori_loop(0, SSD_NCHUNK, backward_chunk, 0)

    rows = 256
    for i in range(SEQ // rows):
        sl = slice(rows * i, rows * (i + 1))
        o_ref[sl, :] = (y_scr[sl, :] + dexp_ref[...] * xs_scr[sl, :]).astype(o_ref.dtype)


def _ssd(xbc, dt, dtt, sw):
    t = xbc.shape[0]
    nb = t // SEQ
    row = lambda n: pl.BlockSpec((SEQ, n), lambda b: (b, 0))
    names = ("cw", "cb", "a", "at", "db", "dbt", "dexp", "ef", "eb", "tril", "triu")
    return pl.pallas_call(
        _ssd_kernel,
        grid=(nb,),
        in_specs=[row(SSD_XBC), row(2 * SSD_HEADS),
                  pl.BlockSpec((2 * SSD_HEADS, SEQ), lambda b: (0, b))]
                 + [_const_spec(sw[n].shape) for n in names],
        out_specs=row(SSD_WIDTH),
        out_shape=jax.ShapeDtypeStruct((t, SSD_WIDTH), BF16),
        scratch_shapes=[pltpu.VMEM((SEQ, SSD_WIDTH), BF16),
                        pltpu.VMEM((SEQ, 2 * SSD_BC), BF16),
                        pltpu.VMEM((SEQ, SSD_WIDTH), F32),
                        pltpu.VMEM((SSD_GROUPS, SSD_STATE, SSD_WIDTH // SSD_GROUPS), F32),
                        pltpu.VMEM((SEQ, 2 * SSD_HEADS), F32),
                        pltpu.VMEM((SEQ, 2 * SSD_HEADS), F32),
                        pltpu.VMEM((SSD_NCHUNK, 2 * SSD_HEADS, SSD_CHUNK), F32),
                        pltpu.VMEM((SSD_NCHUNK, 2 * SSD_HEADS, SSD_CHUNK), F32)],
        compiler_params=_cparams(("arbitrary",)),
        name="ssd",
    )(xbc, dt, dtt, *[sw[n] for n in names])


def _merge_kernel(x_ref, ya_ref, yb_ref, yc_ref, z_ref, g_ref, pa_ref, pb_ref, pc_ref, wo_ref, nc_ref, nw_ref,
                  xo_ref, h_ref):
    gate = jax.nn.sigmoid(g_ref[...].astype(F32))
    yb = jnp.concatenate([yb_ref[j] for j in range(S5_SLABS)], axis=1).astype(BF16)
    zz = z_ref[...].astype(F32)
    yc = _rms(yc_ref[...].astype(F32) * (zz * jax.nn.sigmoid(zz)), nc_ref[...]).astype(BF16)
    merged = (gate[:, 0:D_MODEL] * _dot(ya_ref[...], pa_ref[...])
              + gate[:, D_MODEL:2 * D_MODEL] * _dot(yb, pb_ref[...])
              + gate[:, 2 * D_MODEL:] * _dot(yc, pc_ref[...]))
    xn = x_ref[...] + _dot(merged.astype(BF16), wo_ref[...])
    xo_ref[...] = xn
    h_ref[...] = _rms(xn, nw_ref[...]).astype(BF16)


def _merge(x2, ya, yb, yc, z, gates, mw):
    t = x2.shape[0]
    tm = PROJ_TM
    row = lambda n: pl.BlockSpec((tm, n), lambda i: (i, 0))
    return pl.pallas_call(
        _merge_kernel,
        grid=(t // tm,),
        in_specs=[row(D_MODEL), row(HY_WIDTH), pl.BlockSpec((S5_SLABS, tm, 128), lambda i: (0, i, 0)),
                  row(SSD_WIDTH), row(SSD_WIDTH), row(3 * D_MODEL),
                  _const_spec(mw["pa"].shape), _const_spec(mw["pb"].shape), _const_spec(mw["pc"].shape),
                  _const_spec(mw["wo"].shape), _const_spec((1, SSD_WIDTH)), _const_spec((1, D_MODEL))],
        out_specs=[row(D_MODEL), row(D_MODEL)],
        out_shape=[jax.ShapeDtypeStruct((t, D_MODEL), F32), jax.ShapeDtypeStruct((t, D_MODEL), BF16)],
        compiler_params=_cparams(("arbitrary",)),
        name="merge",
    )(x2, ya, yb, yc, z, gates, mw["pa"], mw["pb"], mw["pc"], mw["wo"], mw["nc"], mw["nw"])


def _ffn_kernel(h_ref, x_ref, wg_ref, wv_ref, cwg_ref, cwv_ref, cbg_ref, cbv_ref, wd_ref, nf_ref, o_ref,
                *, final_norm):
    j = pl.program_id(1)
    row = lax.broadcasted_iota(jnp.int32, (SEQ, FFN_TILE), 0)
    h = h_ref[...]

    def conv(u, w, b):
        return b + w[0:1] * _shift_rows(u, -1, row) + w[1:2] * u + w[2:3] * _shift_rows(u, 1, row)

    gate = conv(_dot(h, wg_ref[...]), cwg_ref[...], cbg_ref[...])
    val = conv(_dot(h, wv_ref[...]), cwv_ref[...], cbv_ref[...])
    act = (gate * jax.nn.sigmoid(gate) * val).astype(BF16)

    rows = 512
    for i in range(SEQ // rows):
        sl = slice(rows * i, rows * (i + 1))
        part = _dot(act[sl], wd_ref[...])

        @pl.when(j == 0)
        def _():
            o_ref[sl, :] = x_ref[sl, :] + part

        @pl.when(j > 0)
        def _():
            o_ref[sl, :] += part

    if final_norm:
        @pl.when(j == FFN_NTILE - 1)
        def _():
            for i in range(SEQ // rows):
                sl = slice(rows * i, rows * (i + 1))
                o_ref[sl, :] = _rms(o_ref[sl, :], nf_ref[...])


def _ffn(h2, x2, fw, nf, final_norm):
    t = x2.shape[0]
    nb = t // SEQ
    seq = lambda n, **kw: pl.BlockSpec((SEQ, n), lambda b, j: (b, 0), **kw)
    tile = lambda r, off: pl.BlockSpec((r, FFN_TILE), lambda b, j, off=off: (0, off + j))
    return pl.pallas_call(
        functools.partial(_ffn_kernel, final_norm=final_norm),
        grid=(nb, FFN_NTILE),
        in_specs=[seq(D_MODEL), seq(D_MODEL, pipeline_mode=pl.Buffered(1)),
                  tile(D_MODEL, 0), tile(D_MODEL, FFN_NTILE),
                  tile(3, 0), tile(3, FFN_NTILE), tile(1, 0), tile(1, FFN_NTILE),
                  pl.BlockSpec((FFN_TILE, D_MODEL), lambda b, j: (j, 0)),
                  pl.BlockSpec((1, D_MODEL), lambda b, j: (0, 0))],
        out_specs=seq(D_MODEL),
        out_shape=jax.ShapeDtypeStruct((t, D_MODEL), F32),
        compiler_params=_cparams(("arbitrary", "arbitrary")),
        name="conv_ffn",
    )(h2, x2, fw["up"], fw["up"], fw["cw"], fw["cw"], fw["cb"], fw["cb"], fw["down"], nf)


_HI = lax.Precision.HIGHEST


def _prep_proj(w_in):
    o = [0, 3 * HY_WIDTH, 3 * HY_WIDTH + S5_WIDTH]
    o.append(o[-1] + SSD_WIDTH)
    o.append(o[-1] + SSD_XBC)
    o.append(o[-1] + 2 * SSD_HEADS)
    wb = w_in.astype(BF16)
    return {"hy": wb[:, o[0]:o[1]], "s5": wb[:, o[1]:o[2]], "z": wb[:, o[2]:o[3]], "xbc": wb[:, o[3]:o[4]],
            "dt": wb[:, o[4]:o[5]], "dtt": wb[:, o[4]:o[5]].T, "gate": wb[:, o[5]:]}


def _prep_hyena(short_w, short_b, w1, b1, freq, w2, b2, w3, decay, bias):
    pos = jnp.arange(SEQ, dtype=F32)
    tt = (pos / max(SEQ - 1, 1))[:, None]
    bands = jnp.linspace(1e-4, HY_BANDS - 1, HY_BANDS, dtype=F32)
    ang = (2.0 * math.pi / SEQ) * pos[:, None] * bands[None, :]
    zz = jnp.concatenate([tt, jnp.cos(ang), -jnp.sin(ang)], axis=-1)
    h = jnp.sin(freq[0] * (jnp.dot(zz, w1, precision=_HI) + b1))
    h = jnp.sin(freq[1] * (jnp.dot(h, w2, precision=_HI) + b2))
    h = jnp.dot(h, w3, precision=_HI) * jnp.exp(-tt * jnp.abs(decay))
    h = h.reshape(SEQ, 2, HY_WIDTH)
    k = jnp.concatenate([h[:, 0], jnp.zeros((1, HY_WIDTH), F32), h[:0:-1, 1]], axis=0)
    k = k / jnp.sum(jnp.abs(k), axis=0, keepdims=True)
    kf = jnp.fft.rfft(k, axis=0)
    wgt = jnp.where(jnp.arange(SEQ) == 0, 1.0, 2.0)[:, None] / NFFT
    return {"sw": short_w.reshape(3, 3, HY_WIDTH).transpose(1, 0, 2),
            "sb": short_b.reshape(3, 1, HY_WIDTH),
            "kre": jnp.real(kf[:SEQ]) * wgt, "kim": jnp.imag(kf[:SEQ]) * wgt,
            "knyq": jnp.real(kf[SEQ:]) / NFFT, "bias": bias.reshape(1, HY_WIDTH)}


def _prep_s5(a_re, a_im, log_dt, b_re, b_im, c_re, c_im, d, w_glu, b_glu):
    q, gs, ns, ng = S5_CHUNK, S5_GROUP, S5_STATE, S5_GROUPS
    dt = jnp.exp(log_dt)[:, :, None]
    mag = jnp.exp(a_re * dt)
    ab_r, ab_i = mag * jnp.cos(a_im * dt), mag * jnp.sin(a_im * dt)
    den = a_re * a_re + a_im * a_im
    f_r = ((ab_r - 1.0) * a_re + ab_i * a_im) / den
    f_i = (ab_i * a_re - (ab_r - 1.0) * a_im) / den
    fb_r = f_r[..., None] * b_re - f_i[..., None] * b_im
    fb_i = f_r[..., None] * b_im + f_i[..., None] * b_re

    def power(j):
        m = jnp.exp(j * a_re * dt)
        return m * jnp.cos(j * a_im * dt), m * jnp.sin(j * a_im * dt)

    p_r, p_i = zip(*[power(float(j)) for j in range(q + 1)])
    p_r, p_i = jnp.stack(p_r, 2), jnp.stack(p_i, 2)

    cp_r = c_re[:, :, None] * p_r[:, :, :q, None] - c_im[:, :, None] * p_i[:, :, :q, None]
    cp_i = c_re[:, :, None] * p_i[:, :, :q, None] + c_im[:, :, None] * p_r[:, :, :q, None]
    kd = (jnp.einsum("zgdcn,zgne->zgdce", cp_r, fb_r, precision=_HI)
          - jnp.einsum("zgdcn,zgne->zgdce", cp_i, fb_i, precision=_HI))
    lag = jnp.arange(q)[:, None] - jnp.arange(q)[None, :]
    tf = jnp.where((lag >= 0)[None, :, :, None, None], kd[0][:, jnp.clip(lag, 0, q - 1)], 0.0)
    tb = jnp.where((lag <= 0)[None, :, :, None, None], kd[1][:, jnp.clip(-lag, 0, q - 1)], 0.0)
    toep = (tf + tb).transpose(0, 1, 3, 2, 4).reshape(ng, q * gs, q * gs)

    def inject(z, pw):
        pr, pi = p_r[z][:, pw], p_i[z][:, pw]
        re = pr[..., None] * fb_r[z][:, None] - pi[..., None] * fb_i[z][:, None]
        im = pr[..., None] * fb_i[z][:, None] + pi[..., None] * fb_r[z][:, None]
        to_rows = lambda v: v.transpose(0, 2, 1, 3).reshape(ng, ns, q * gs)
        return to_rows(re), to_rows(im)

    bf_r, bf_i = inject(0, jnp.arange(q - 1, -1, -1))
    bb_r, bb_i = inject(1, jnp.arange(q))
    tbm = jnp.concatenate([toep, bf_r, bb_r, bf_i, bb_i], axis=1).astype(BF16)

    def readout(z, pw):
        pr, pi = p_r[z][:, pw], p_i[z][:, pw]
        re = c_re[z][:, None] * pr[:, :, None] - c_im[z][:, None] * pi[:, :, None]
        im = c_re[z][:, None] * pi[:, :, None] + c_im[z][:, None] * pr[:, :, None]
        return re.reshape(ng, q * gs, ns), -im.reshape(ng, q * gs, ns)

    cf_r, cf_i = readout(0, jnp.arange(1, q + 1))
    cb_r, cb_i = readout(1, jnp.arange(q, 0, -1))
    cgm = jnp.concatenate([cf_r, cb_r, cf_i, cb_i], axis=2).astype(BF16)

    cols = []
    for lvl in range(7):
        pr, pi = power(float(q << lvl))
        cols += [jnp.concatenate([pr[0], pr[1]], axis=1), jnp.concatenate([pi[0], pi[1]], axis=1)]
    cols += [jnp.zeros_like(cols[0])] * 2
    sc = jnp.stack(cols, axis=2)
    ones = jnp.ones((1, S5_NCHUNK), F32)
    return {"tb": tbm, "cg": cgm, "sc": sc, "dcol": d[:, None] * ones, "wgt": w_glu.T.astype(BF16),
            "bcol": b_glu[:, None] * ones}


def _prep_ssd(conv_w, conv_b, a_log, dt_bias, d):
    nh = SSD_HEADS
    a = (-jnp.exp(a_log)).reshape(1, 2 * nh)
    db = dt_bias.reshape(1, 2 * nh)
    head_of_lane = jnp.arange(SSD_WIDTH) // SSD_HEADDIM
    onehot = (jnp.arange(nh)[:, None] == head_of_lane[None, :]).astype(BF16)
    zero = jnp.zeros_like(onehot)
    i = jnp.arange(SSD_CHUNK)
    return {"cw": conv_w, "cb": conv_b.reshape(1, SSD_XBC), "a": a, "at": a.T, "db": db, "dbt": db.T,
            "dexp": jnp.repeat(d, SSD_HEADDIM).reshape(1, SSD_WIDTH),
            "ef": jnp.concatenate([onehot, zero], 0), "eb": jnp.concatenate([zero, onehot], 0),
            "tril": (i[:, None] >= i[None, :]).astype(BF16), "triu": (i[:, None] <= i[None, :]).astype(BF16)}


def kernel(x, norm_mix, w_in, hy_short_w, hy_short_b, hy_w1, hy_b1, hy_freq, hy_w2, hy_b2, hy_w3, hy_decay, hy_bias, s5_a_re, s5_a_im, s5_log_dt, s5_b_re, s5_b_im, s5_c_re, s5_c_im, s5_d, s5_w_glu, s5_b_glu, ssd_conv_w, ssd_conv_b, ssd_a_log, ssd_dt_bias, ssd_d, ssd_norm, p_a, p_b, p_c, w_out, norm_ffn, ffn_up, ffn_conv_w, ffn_conv_b, ffn_down, norm_final):
    bsz, seq, dm = x.shape
    assert (seq, dm) == (SEQ, D_MODEL)
    x2 = x.reshape(bsz * seq, dm)
    fc, fs = _dft_matrices()
    nf = norm_final.reshape(1, D_MODEL)
    for l in range(DEPTH):
        pw = _prep_proj(w_in[l])
        hw = _prep_hyena(hy_short_w[l], hy_short_b[l], hy_w1[l], hy_b1[l], hy_freq[l], hy_w2[l], hy_b2[l],
                         hy_w3[l], hy_decay[l], hy_bias[l])
        s5w = _prep_s5(s5_a_re[l], s5_a_im[l], s5_log_dt[l], s5_b_re[l], s5_b_im[l], s5_c_re[l], s5_c_im[l],
                       s5_d[l], s5_w_glu[l], s5_b_glu[l])
        sdw = _prep_ssd(ssd_conv_w[l], ssd_conv_b[l], ssd_a_log[l], ssd_dt_bias[l], ssd_d[l])
        mw = {"pa": p_a[l].astype(BF16), "pb": p_b[l].astype(BF16), "pc": p_c[l].astype(BF16),
              "wo": w_out[l].astype(BF16), "nc": ssd_norm[l].reshape(1, SSD_WIDTH),
              "nw": norm_ffn[l].reshape(1, D_MODEL)}
        fw = {"up": ffn_up[l].astype(BF16), "cw": ffn_conv_w[l], "cb": ffn_conv_b[l].reshape(1, -1),
              "down": ffn_down[l].astype(BF16)}

        hy_u, s5_u, z, xbc, dt, dtt, gates = _proj(x2, norm_mix[l].reshape(1, D_MODEL), pw)
        ya = _hyena(hy_u, hw, fc, fs)
        yb = _s5(s5_u, s5w)
        yc = _ssd(xbc, dt, dtt, sdw)
        x_mid, h2 = _merge(x2, ya, yb, yc, z, gates, mw)
        x2 = _ffn(h2, x_mid, fw, nf, final_norm=(l == DEPTH - 1))
    return x2.reshape(bsz, seq, dm)
```

```python
import functools
import math

import jax
import jax.numpy as jnp
from jax import lax
from jax.experimental import pallas as pl
from jax.experimental.pallas import tpu as pltpu

F32 = jnp.float32
BF16 = jnp.bfloat16

D_MODEL = 1024
SEQ = 2048
DEPTH = 2
EPS = 1e-6

HY_WIDTH = 512
HY_BANDS = 16
NFFT = 2 * SEQ

S5_WIDTH = 512
S5_GROUP = 16
S5_GROUPS = S5_WIDTH // S5_GROUP
S5_STATE = 64
S5_CHUNK = 16
S5_NCHUNK = SEQ // S5_CHUNK
S5_SLABS = S5_WIDTH // 128

SSD_WIDTH = 1024
SSD_HEADDIM = 64
SSD_HEADS = SSD_WIDTH // SSD_HEADDIM
SSD_GROUPS = 2
SSD_STATE = 128
SSD_BC = SSD_GROUPS * SSD_STATE
SSD_XBC = SSD_WIDTH + 2 * SSD_BC
SSD_CONV = 5
SSD_CHUNK = 128
SSD_NCHUNK = SEQ // SSD_CHUNK
HEADS_PER_GROUP = SSD_HEADS // SSD_GROUPS

FFN_HIDDEN = 2816
FFN_TILE = 256
FFN_NTILE = FFN_HIDDEN // FFN_TILE
FFN_HALO = 16

PROJ_TM = 512
HY_TILE = 256
CONV_PAD = 8
VMEM_LIMIT = 56 * 1024 * 1024


def _cparams(sem):
    return pltpu.CompilerParams(dimension_semantics=sem, vmem_limit_bytes=VMEM_LIMIT)


def _const_spec(shape):
    nd = len(shape)
    return pl.BlockSpec(shape, lambda *_: (0,) * nd, pipeline_mode=pl.Buffered(1))


def _rms(x, w):
    return x * lax.rsqrt(jnp.mean(x * x, axis=-1, keepdims=True) + EPS) * w


def _dot(a, b):
    return jnp.dot(a, b, preferred_element_type=F32)


def _dot_nt(a, b):
    return lax.dot_general(a, b, (((1,), (1,)), ((), ())), preferred_element_type=F32)


def _dot_tn(a, b):
    return lax.dot_general(a, b, (((0,), (0,)), ((), ())), preferred_element_type=F32)


def _split3(a):
    a1 = a.astype(BF16)
    r1 = a - a1.astype(F32)
    a2 = r1.astype(BF16)
    a3 = (r1 - a2.astype(F32)).astype(BF16)
    return a1, a2, a3


def _shift_rows(x, j, row):
    n = x.shape[0]
    rolled = pltpu.roll(x, (-j) % n, 0)
    valid = (row >= -j) if j < 0 else (row < n - j)
    return jnp.where(valid, rolled, 0.0)


def _dft_kernel(fc_ref, fs_ref):
    rows = fc_ref.shape[0]
    r = lax.broadcasted_iota(jnp.int32, (rows, SEQ), 0) + pl.program_id(0) * rows
    c = lax.broadcasted_iota(jnp.int32, (rows, SEQ), 1)
    ang = ((r * c) & (NFFT - 1)).astype(F32) * (2.0 * math.pi / NFFT)
    fc_ref[...] = jnp.cos(ang).astype(BF16)
    fs_ref[...] = jnp.sin(ang).astype(BF16)


def _dft_matrices():
    rows = 256
    return pl.pallas_call(
        _dft_kernel,
        grid=(SEQ // rows,),
        out_specs=[pl.BlockSpec((rows, SEQ), lambda i: (i, 0))] * 2,
        out_shape=[jax.ShapeDtypeStruct((SEQ, SEQ), BF16)] * 2,
        compiler_params=_cparams(("arbitrary",)),
        name="dft_matrices",
    )()


def _proj_kernel(x_ref, nw_ref, why_ref, ws5_ref, wz_ref, wxbc_ref, wdt_ref, wdtt_ref, wg_ref,
                 hy_ref, s5_ref, z_ref, xbc_ref, dt_ref, dtt_ref, g_ref):
    h = _rms(x_ref[...], nw_ref[...]).astype(BF16)
    hy_ref[...] = _dot(h, why_ref[...]).astype(BF16)
    u = _dot(h, ws5_ref[...])
    for j in range(S5_SLABS):
        s5_ref[j] = u[:, 128 * j:128 * (j + 1)]
    z_ref[...] = _dot(h, wz_ref[...]).astype(BF16)
    xbc_ref[...] = _dot(h, wxbc_ref[...]).astype(BF16)
    dt_ref[...] = _dot(h, wdt_ref[...])
    dtt_ref[...] = _dot_nt(wdtt_ref[...], h)
    g_ref[...] = _dot(h, wg_ref[...]).astype(BF16)


def _proj(x2, nw, w):
    t = x2.shape[0]
    tm = PROJ_TM
    row = lambda n: pl.BlockSpec((tm, n), lambda i: (i, 0))
    return pl.pallas_call(
        _proj_kernel,
        grid=(t // tm,),
        in_specs=[row(D_MODEL), _const_spec((1, D_MODEL)),
                  _const_spec(w["hy"].shape), _const_spec(w["s5"].shape), _const_spec(w["z"].shape),
                  _const_spec(w["xbc"].shape), _const_spec(w["dt"].shape), _const_spec(w["dtt"].shape),
                  _const_spec(w["gate"].shape)],
        out_specs=[row(3 * HY_WIDTH),
                   pl.BlockSpec((S5_SLABS, tm, 128), lambda i: (0, i, 0)),
                   row(SSD_WIDTH), row(SSD_XBC), row(2 * SSD_HEADS),
                   pl.BlockSpec((2 * SSD_HEADS, tm), lambda i: (0, i)),
                   row(3 * D_MODEL)],
        out_shape=[jax.ShapeDtypeStruct((t, 3 * HY_WIDTH), BF16),
                   jax.ShapeDtypeStruct((S5_SLABS, t, 128), F32),
                   jax.ShapeDtypeStruct((t, SSD_WIDTH), BF16),
                   jax.ShapeDtypeStruct((t, SSD_XBC), BF16),
                   jax.ShapeDtypeStruct((t, 2 * SSD_HEADS), F32),
                   jax.ShapeDtypeStruct((2 * SSD_HEADS, t), F32),
                   jax.ShapeDtypeStruct((t, 3 * D_MODEL), BF16)],
        compiler_params=_cparams(("arbitrary",)),
        name="in_proj",
    )(x2, nw, w["hy"], w["s5"], w["z"], w["xbc"], w["dt"], w["dtt"], w["gate"])


def _hyena_kernel(x0_ref, x1_ref, v_ref, sw_ref, sb_ref, fc_ref, fs_ref, kre_ref, kim_ref, knyq_ref,
                  bias_ref, o_ref):
    row = lax.broadcasted_iota(jnp.int32, (SEQ, HY_TILE), 0)

    def short_conv(ref, p):
        u = ref[...].astype(F32)
        w = sw_ref[p]
        return (sb_ref[p] + w[0:1] * _shift_rows(u, -1, row) + w[1:2] * u
                + w[2:3] * _shift_rows(u, 1, row))

    x1 = short_conv(x1_ref, 1)
    v = short_conv(v_ref, 2)
    s = v * x1
    sb = s.astype(BF16)
    sre = _dot(fc_ref[...], sb)
    sim = _dot(fs_ref[...], sb)
    kre = kre_ref[...]
    kim = kim_ref[...]
    yre = (sre * kre + sim * kim).astype(BF16)
    yim = (sim * kre - sre * kim).astype(BF16)
    y = _dot(fc_ref[...], yre) + _dot(fs_ref[...], yim)
    alt = jnp.where((row & 1) == 0, 1.0, -1.0)
    snyq = jnp.sum(s * alt, axis=0, keepdims=True)
    y = y + alt * (snyq * knyq_ref[...])
    x0 = short_conv(x0_ref, 0)
    o_ref[...] = (x0 * (y + bias_ref[...] * s)).astype(o_ref.dtype)


def _hyena(hy_u, hw, fc, fs):
    t = hy_u.shape[0]
    nb = t // SEQ
    nj = HY_WIDTH // HY_TILE
    part = lambda p: pl.BlockSpec((SEQ, HY_TILE), lambda j, b, p=p: (b, p * nj + j))
    ctile = lambda r: pl.BlockSpec((r, HY_TILE), lambda j, b: (0, j))
    return pl.pallas_call(
        _hyena_kernel,
        grid=(nj, nb),
        in_specs=[part(0), part(1), part(2),
                  pl.BlockSpec((3, 3, HY_TILE), lambda j, b: (0, 0, j)),
                  pl.BlockSpec((3, 1, HY_TILE), lambda j, b: (0, 0, j)),
                  _const_spec((SEQ, SEQ)), _const_spec((SEQ, SEQ)),
                  ctile(SEQ), ctile(SEQ), ctile(1), ctile(1)],
        out_specs=pl.BlockSpec((SEQ, HY_TILE), lambda j, b: (b, j)),
        out_shape=jax.ShapeDtypeStruct((t, HY_WIDTH), BF16),
        compiler_params=_cparams(("arbitrary", "arbitrary")),
        name="hyena",
    )(hy_u, hy_u, hy_u, hw["sw"], hw["sb"], fc, fs, hw["kre"], hw["kim"], hw["knyq"], hw["bias"])


def _s5_kernel(u_ref, tb_ref, cg_ref, sc_ref, dcol_ref, wgt_ref, bcol_ref, o_ref, ut_scr, yt_scr):
    nk = S5_NCHUNK
    for t in range(S5_CHUNK):
        for j in range(S5_SLABS):
            blk = u_ref[j, pl.ds(t, nk, stride=S5_CHUNK), :]
            ut_scr[t, 128 * j:128 * (j + 1), :] = blk.T

    half = S5_STATE
    row = lax.broadcasted_iota(jnp.int32, (nk, 2 * half), 0)

    def from_prev(x, s):
        if s % 8 == 0:
            return jnp.concatenate([jnp.zeros((s, 2 * half), F32), x[:nk - s]], axis=0)
        return jnp.where(row >= s, pltpu.roll(x, s, 0), 0.0)

    def from_next(x, s):
        if s % 8 == 0:
            return jnp.concatenate([x[s:], jnp.zeros((s, 2 * half), F32)], axis=0)
        return jnp.where(row < nk - s, pltpu.roll(x, nk - s, 0), 0.0)

    def group_pair(p, carry):
        rs = []
        for i in range(2):
            r0 = pl.multiple_of((2 * p + i) * S5_GROUP, S5_GROUP)
            ucol = ut_scr[:, pl.ds(r0, S5_GROUP), :].reshape(S5_CHUNK * S5_GROUP, nk).astype(BF16)
            rs.append(_dot(tb_ref[2 * p + i], ucol))

        def states(lo):
            return jnp.concatenate([rs[0][lo:lo + half], rs[1][lo:lo + half]], axis=0).T

        fr, br, fi, bi = states(256), states(256 + half), states(256 + 2 * half), states(256 + 3 * half)
        sc = sc_ref[p]
        for lvl in range(7):
            s = 1 << lvl
            arf, aif, arb, aib = (sc[4 * lvl + i:4 * lvl + i + 1] for i in range(4))
            pfr, pfi = from_prev(fr, s), from_prev(fi, s)
            pbr, pbi = from_next(br, s), from_next(bi, s)
            fr, fi = fr + arf * pfr - aif * pfi, fi + arf * pfi + aif * pfr
            br, bi = br + arb * pbr - aib * pbi, bi + arb * pbi + aib * pbr
        carried = [from_prev(fr, 1).T, from_next(br, 1).T, from_prev(fi, 1).T, from_next(bi, 1).T]
        for i in range(2):
            r0 = pl.multiple_of((2 * p + i) * S5_GROUP, S5_GROUP)
            hs = jnp.concatenate([c[half * i:half * (i + 1)] for c in carried], axis=0).astype(BF16)
            z = rs[i][0:256] + _dot(cg_ref[2 * p + i], hs)
            yt_scr[:, pl.ds(r0, S5_GROUP), :] = z.reshape(S5_CHUNK, S5_GROUP, nk)
        return carry

    lax.fori_loop(0, S5_GROUPS // 2, group_pair, 0)

    for t in range(S5_CHUNK):
        gl = jax.nn.gelu(yt_scr[t] + dcol_ref[...] * ut_scr[t])
        z2 = _dot(wgt_ref[...], gl.astype(BF16)) + bcol_ref[...]
        yb = gl * jax.nn.sigmoid(z2)
        for j in range(S5_SLABS):
            o_ref[j, pl.ds(t, nk, stride=S5_CHUNK), :] = yb[128 * j:128 * (j + 1), :].T


def _s5(s5_u, sw):
    t = s5_u.shape[1]
    nb = t // SEQ
    slab = pl.BlockSpec((S5_SLABS, SEQ, 128), lambda b: (0, b, 0))
    return pl.pallas_call(
        _s5_kernel,
        grid=(nb,),
        in_specs=[slab, _const_spec(sw["tb"].shape), _const_spec(sw["cg"].shape), _const_spec(sw["sc"].shape),
                  _const_spec(sw["dcol"].shape), _const_spec(sw["wgt"].shape), _const_spec(sw["bcol"].shape)],
        out_specs=slab,
        out_shape=jax.ShapeDtypeStruct((S5_SLABS, t, 128), F32),
        scratch_shapes=[pltpu.VMEM((S5_CHUNK, S5_WIDTH, S5_NCHUNK), F32),
                        pltpu.VMEM((S5_CHUNK, S5_WIDTH, S5_NCHUNK), F32)],
        compiler_params=_cparams(("arbitrary",)),
        name="s5",
    )(s5_u, sw["tb"], sw["cg"], sw["sc"], sw["dcol"], sw["wgt"], sw["bcol"])


def _ssd_kernel(xbc_ref, dt_ref, dtt_ref, cw_ref, cb_ref, a_ref, at_ref, db_ref, dbt_ref, dexp_ref,
                ef_ref, eb_ref, tril_ref, triu_ref, o_ref,
                xs_scr, bc_scr, y_scr, carry_scr, dt_scr, a_scr, dtt_scr, att_scr, pad_scr):
    q = SSD_CHUNK
    nh = SSD_HEADS
    cblk = pad_scr.shape[1]

    pad_scr[0:CONV_PAD, :] = jnp.zeros((CONV_PAD, cblk), F32)
    pad_scr[CONV_PAD + SEQ:, :] = jnp.zeros((CONV_PAD, cblk), F32)
    rc = 256
    for i in range(SSD_XBC // cblk):
        c0 = cblk * i
        pad_scr[CONV_PAD:CONV_PAD + SEQ, :] = xbc_ref[:, c0:c0 + cblk].astype(F32)
        w = cw_ref[:, c0:c0 + cblk]
        bias = cb_ref[:, c0:c0 + cblk]
        dst, d0 = (xs_scr, c0) if c0 < SSD_WIDTH else (bc_scr, c0 - SSD_WIDTH)

        def conv_rows(r, carry, w=w, bias=bias, dst=dst, d0=d0):
            r0 = pl.multiple_of(r * rc, rc)
            acc = bias
            for k in range(SSD_CONV):
                acc = acc + w[k:k + 1] * pad_scr[pl.ds(r0 + (CONV_PAD + k - SSD_CONV // 2), rc), :]
            dst[pl.ds(r0, rc), d0:d0 + cblk] = (acc * jax.nn.sigmoid(acc)).astype(BF16)
            return carry

        lax.fori_loop(0, SEQ // rc, conv_rows, 0)

    dtv = jax.nn.softplus(dt_ref[...] + db_ref[...])
    dt_scr[...] = dtv
    a_scr[...] = dtv * a_ref[...]
    dtvt = jax.nn.softplus(dtt_ref[...] + dbt_ref[...])
    avt = dtvt * at_ref[...]
    for c in range(SSD_NCHUNK):
        dtt_scr[c] = dtvt[:, q * c:q * (c + 1)]
        att_scr[c] = avt[:, q * c:q * (c + 1)]

    ri = lax.broadcasted_iota(jnp.int32, (q, q), 0)
    ci = lax.broadcasted_iota(jnp.int32, (q, q), 1)
    causal = ri >= ci
    anti = ci >= ri
    col_is_f = lax.broadcasted_iota(jnp.int32, (q, 2 * nh), 1) < nh
    row_is_f = lax.broadcasted_iota(jnp.int32, (2 * nh, q), 0) < nh
    lane_lo = lax.broadcasted_iota(jnp.int32, (q, 128), 1) < SSD_HEADDIM
    tril = tril_ref[...]
    triu = triu_ref[...]

    def col_cumsum(a_c):
        parts = _split3(a_c)
        lo = sum(_dot(tril, p) for p in parts)
        hi = sum(_dot(triu, p) for p in parts)
        return jnp.where(col_is_f, lo, hi)

    def row_cumsum(at_c):
        parts = _split3(at_c)
        lo = sum(_dot(p, triu) for p in parts)
        hi = sum(_dot(p, tril) for p in parts)
        return jnp.where(row_is_f, lo, hi)

    def expand(v, e_ref):
        v1 = v.astype(BF16)
        v2 = (v - v1.astype(F32)).astype(BF16)
        return _dot(v1, e_ref[...]) + _dot(v2, e_ref[...])

    def inter_chunk(c, e_ref, edge):
        r0 = pl.multiple_of(c * q, q)
        cs = col_cumsum(a_scr[pl.ds(r0, q), :])
        total = jnp.where(col_is_f[0:1], cs[q - 1:q], cs[0:1])
        w_state = expand(dt_scr[pl.ds(r0, q), :] * jnp.exp(total - cs), e_ref)
        w_in = expand(jnp.exp(cs), e_ref)
        xdd = (xs_scr[pl.ds(r0, q), :] * w_state).astype(BF16)
        outs = []
        for g in range(SSD_GROUPS):
            bg = bc_scr[pl.ds(r0, q), SSD_STATE * g:SSD_STATE * (g + 1)]
            cg = bc_scr[pl.ds(r0, q), SSD_BC + SSD_STATE * g:SSD_BC + SSD_STATE * (g + 1)]
            lanes = slice(512 * g, 512 * (g + 1))
            prev = carry_scr[g]
            outs.append(_dot(cg, prev.astype(BF16)) * w_in[:, lanes])
            carry_scr[g] = prev * w_in[edge:edge + 1, lanes] + _dot_tn(bg, xdd[:, lanes])
        return r0, jnp.concatenate(outs, axis=1)

    def forward_chunk(c, carry):
        r0 = pl.multiple_of(c * q, q)
        cs = col_cumsum(a_scr[pl.ds(r0, q), :])
        rs = row_cumsum(att_scr[c])
        dtt_c = dtt_scr[c]
        xs_b = xs_scr[pl.ds(r0, q), :]
        pairs = []
        for g in range(SSD_GROUPS):
            bg = bc_scr[pl.ds(r0, q), SSD_STATE * g:SSD_STATE * (g + 1)]
            cg = bc_scr[pl.ds(r0, q), SSD_BC + SSD_STATE * g:SSD_BC + SSD_STATE * (g + 1)]
            scores = _dot_nt(cg, bg)
            for j in range(HEADS_PER_GROUP // 2):
                ms = []
                for h in (HEADS_PER_GROUP * g + 2 * j, HEADS_PER_GROUP * g + 2 * j + 1):
                    hb = nh + h
                    lf = jnp.where(causal, jnp.exp(cs[:, h:h + 1] - rs[h:h + 1, :]), 0.0) * dtt_c[h:h + 1, :]
                    lb = jnp.where(anti, jnp.exp(cs[:, hb:hb + 1] - rs[hb:hb + 1, :]), 0.0) * dtt_c[hb:hb + 1, :]
                    ms.append((scores * (lf + lb)).astype(BF16))
                blk = HEADS_PER_GROUP // 2 * g + j
                xp = xs_b[:, 128 * blk:128 * (blk + 1)]
                rhs = jnp.concatenate([jnp.where(lane_lo, xp, 0), jnp.where(lane_lo, 0, xp)], axis=0)
                pairs.append(_dot(jnp.concatenate(ms, axis=1), rhs))
        _, y_off = inter_chunk(c, ef_ref, q - 1)
        y_scr[pl.ds(r0, q), :] = jnp.concatenate(pairs, axis=1) + y_off
        return carry

    def backward_chunk(i, carry):
        r0, y_off = inter_chunk(SSD_NCHUNK - 1 - i, eb_ref, 0)
        y_scr[pl.ds(r0, q), :] += y_off
        return carry

    carry_scr[...] = jnp.zeros_like(carry_scr)
    lax.fori_loop(0, SSD_NCHUNK, forward_chunk, 0, unroll=2)
    carry_scr[...] = jnp.zeros_like(carry_scr)
    lax.fori_loop(0, SSD_NCHUNK, backward_chunk, 0, unroll=2)

    rows = 256
    for i in range(SEQ // rows):
        sl = slice(rows * i, rows * (i + 1))
        o_ref[sl, :] = (y_scr[sl, :] + dexp_ref[...] * xs_scr[sl, :]).astype(o_ref.dtype)


def _ssd(xbc, dt, dtt, sw):
    t = xbc.shape[0]
    nb = t // SEQ
    row = lambda n: pl.BlockSpec((SEQ, n), lambda b: (b, 0))
    names = ("cw", "cb", "a", "at", "db", "dbt", "dexp", "ef", "eb", "tril", "triu")
    return pl.pallas_call(
        _ssd_kernel,
        grid=(nb,),
        in_specs=[row(SSD_XBC), row(2 * SSD_HEADS),
                  pl.BlockSpec((2 * SSD_HEADS, SEQ), lambda b: (0, b))]
                 + [_const_spec(sw[n].shape) for n in names],
        out_specs=row(SSD_WIDTH),
        out_shape=jax.ShapeDtypeStruct((t, SSD_WIDTH), BF16),
        scratch_shapes=[pltpu.VMEM((SEQ, SSD_WIDTH), BF16),
                        pltpu.VMEM((SEQ, 2 * SSD_BC), BF16),
                        pltpu.VMEM((SEQ, SSD_WIDTH), F32),
                        pltpu.VMEM((SSD_GROUPS, SSD_STATE, SSD_WIDTH // SSD_GROUPS), F32),
                        pltpu.VMEM((SEQ, 2 * SSD_HEADS), F32),
                        pltpu.VMEM((SEQ, 2 * SSD_HEADS), F32),
                        pltpu.VMEM((SSD_NCHUNK, 2 * SSD_HEADS, SSD_CHUNK), F32),
                        pltpu.VMEM((SSD_NCHUNK, 2 * SSD_HEADS, SSD_CHUNK), F32),
                        pltpu.VMEM((SEQ + 2 * CONV_PAD, 128), F32)],
        compiler_params=_cparams(("arbitrary",)),
        name="ssd",
    )(xbc, dt, dtt, *[sw[n] for n in names])


def _merge_kernel(x_ref, ya_ref, yb_ref, yc_ref, z_ref, g_ref, pa_ref, pb_ref, pc_ref, wo_ref, nc_ref, nw_ref,
                  xo_ref, h_ref):
    gate = jax.nn.sigmoid(g_ref[...].astype(F32))
    yb = jnp.concatenate([yb_ref[j] for j in range(S5_SLABS)], axis=1).astype(BF16)
    zz = z_ref[...].astype(F32)
    yc = _rms(yc_ref[...].astype(F32) * (zz * jax.nn.sigmoid(zz)), nc_ref[...]).astype(BF16)
    merged = (gate[:, 0:D_MODEL] * _dot(ya_ref[...], pa_ref[...])
              + gate[:, D_MODEL:2 * D_MODEL] * _dot(yb, pb_ref[...])
              + gate[:, 2 * D_MODEL:] * _dot(yc, pc_ref[...]))
    xn = x_ref[...] + _dot(merged.astype(BF16), wo_ref[...])
    xo_ref[...] = xn
    h_ref[...] = _rms(xn, nw_ref[...]).astype(BF16)


def _merge(x2, ya, yb, yc, z, gates, mw):
    t = x2.shape[0]
    tm = PROJ_TM
    row = lambda n: pl.BlockSpec((tm, n), lambda i: (i, 0))
    return pl.pallas_call(
        _merge_kernel,
        grid=(t // tm,),
        in_specs=[row(D_MODEL), row(HY_WIDTH), pl.BlockSpec((S5_SLABS, tm, 128), lambda i: (0, i, 0)),
                  row(SSD_WIDTH), row(SSD_WIDTH), row(3 * D_MODEL),
                  _const_spec(mw["pa"].shape), _const_spec(mw["pb"].shape), _const_spec(mw["pc"].shape),
                  _const_spec(mw["wo"].shape), _const_spec((1, SSD_WIDTH)), _const_spec((1, D_MODEL))],
        out_specs=[row(D_MODEL), row(D_MODEL)],
        out_shape=[jax.ShapeDtypeStruct((t, D_MODEL), F32), jax.ShapeDtypeStruct((t, D_MODEL), BF16)],
        compiler_params=_cparams(("arbitrary",)),
        name="merge",
    )(x2, ya, yb, yc, z, gates, mw["pa"], mw["pb"], mw["pc"], mw["wo"], mw["nc"], mw["nw"])


def _ffn_kernel(h_ref, hp_ref, hn_ref, x_ref, wu_ref, cw_ref, cb_ref, wd_ref, nf_ref, o_ref, *, final_norm):
    tm = h_ref.shape[0]
    tiles_per_seq = SEQ // tm
    pos = pl.program_id(0) % tiles_per_seq
    prev = jnp.where(pos == 0, jnp.zeros_like(hp_ref[...]), hp_ref[...])
    nxt = jnp.where(pos == tiles_per_seq - 1, jnp.zeros_like(hn_ref[...]), hn_ref[...])
    hext = jnp.concatenate([prev, h_ref[...], nxt], axis=0)
    ext = tm + 2 * FFN_HALO

    def conv(cols):
        u = _dot(hext, wu_ref[:, cols])
        w = cw_ref[:, cols]
        y = w[0:1] * pltpu.roll(u, 1, 0) + w[1:2] * u + w[2:3] * pltpu.roll(u, ext - 1, 0)
        return y[FFN_HALO:FFN_HALO + tm] + cb_ref[:, cols]

    acts = []
    for j in range(FFN_NTILE):
        gate = conv(slice(FFN_TILE * j, FFN_TILE * (j + 1)))
        val = conv(slice(FFN_HIDDEN + FFN_TILE * j, FFN_HIDDEN + FFN_TILE * (j + 1)))
        acts.append((gate * jax.nn.sigmoid(gate) * val).astype(BF16))
    y = x_ref[...] + _dot(jnp.concatenate(acts, axis=1), wd_ref[...])
    o_ref[...] = _rms(y, nf_ref[...]) if final_norm else y


def _ffn(h2, x2, fw, nf, final_norm):
    t = x2.shape[0]
    tm = PROJ_TM
    hb = tm // FFN_HALO
    last = t // FFN_HALO - 1
    row = lambda n: pl.BlockSpec((tm, n), lambda i: (i, 0))
    return pl.pallas_call(
        functools.partial(_ffn_kernel, final_norm=final_norm),
        grid=(t // tm,),
        in_specs=[row(D_MODEL),
                  pl.BlockSpec((FFN_HALO, D_MODEL), lambda i: (jnp.maximum(i * hb - 1, 0), 0)),
                  pl.BlockSpec((FFN_HALO, D_MODEL), lambda i: (jnp.minimum((i + 1) * hb, last), 0)),
                  row(D_MODEL),
                  _const_spec(fw["up"].shape), _const_spec(fw["cw"].shape), _const_spec(fw["cb"].shape),
                  _const_spec(fw["down"].shape), _const_spec((1, D_MODEL))],
        out_specs=row(D_MODEL),
        out_shape=jax.ShapeDtypeStruct((t, D_MODEL), F32),
        compiler_params=_cparams(("arbitrary",)),
        name="conv_ffn",
    )(h2, h2, h2, x2, fw["up"], fw["cw"], fw["cb"], fw["down"], nf)


_HI = lax.Precision.HIGHEST


def _prep_proj(w_in):
    o = [0, 3 * HY_WIDTH, 3 * HY_WIDTH + S5_WIDTH]
    o.append(o[-1] + SSD_WIDTH)
    o.append(o[-1] + SSD_XBC)
    o.append(o[-1] + 2 * SSD_HEADS)
    wb = w_in.astype(BF16)
    return {"hy": wb[:, o[0]:o[1]], "s5": wb[:, o[1]:o[2]], "z": wb[:, o[2]:o[3]], "xbc": wb[:, o[3]:o[4]],
            "dt": wb[:, o[4]:o[5]], "dtt": wb[:, o[4]:o[5]].T, "gate": wb[:, o[5]:]}


def _prep_hyena(short_w, short_b, w1, b1, freq, w2, b2, w3, decay, bias):
    pos = jnp.arange(SEQ, dtype=F32)
    tt = (pos / max(SEQ - 1, 1))[:, None]
    bands = jnp.linspace(1e-4, HY_BANDS - 1, HY_BANDS, dtype=F32)
    ang = (2.0 * math.pi / SEQ) * pos[:, None] * bands[None, :]
    zz = jnp.concatenate([tt, jnp.cos(ang), -jnp.sin(ang)], axis=-1)
    h = jnp.sin(freq[0] * (jnp.dot(zz, w1, precision=_HI) + b1))
    h = jnp.sin(freq[1] * (jnp.dot(h, w2, precision=_HI) + b2))
    h = jnp.dot(h, w3, precision=_HI) * jnp.exp(-tt * jnp.abs(decay))
    h = h.reshape(SEQ, 2, HY_WIDTH)
    k = jnp.concatenate([h[:, 0], jnp.zeros((1, HY_WIDTH), F32), h[:0:-1, 1]], axis=0)
    k = k / jnp.sum(jnp.abs(k), axis=0, keepdims=True)
    kf = jnp.fft.rfft(k, axis=0)
    wgt = jnp.where(jnp.arange(SEQ) == 0, 1.0, 2.0)[:, None] / NFFT
    return {"sw": short_w.reshape(3, 3, HY_WIDTH).transpose(1, 0, 2),
            "sb": short_b.reshape(3, 1, HY_WIDTH),
            "kre": jnp.real(kf[:SEQ]) * wgt, "kim": jnp.imag(kf[:SEQ]) * wgt,
            "knyq": jnp.real(kf[SEQ:]) / NFFT, "bias": bias.reshape(1, HY_WIDTH)}


def _prep_s5(a_re, a_im, log_dt, b_re, b_im, c_re, c_im, d, w_glu, b_glu):
    q, gs, ns, ng = S5_CHUNK, S5_GROUP, S5_STATE, S5_GROUPS
    dt = jnp.exp(log_dt)[:, :, None]
    mag = jnp.exp(a_re * dt)
    ab_r, ab_i = mag * jnp.cos(a_im * dt), mag * jnp.sin(a_im * dt)
    den = a_re * a_re + a_im * a_im
    f_r = ((ab_r - 1.0) * a_re + ab_i * a_im) / den
    f_i = (ab_i * a_re - (ab_r - 1.0) * a_im) / den
    fb_r = f_r[..., None] * b_re - f_i[..., None] * b_im
    fb_i = f_r[..., None] * b_im + f_i[..., None] * b_re

    def power(j):
        m = jnp.exp(j * a_re * dt)
        return m * jnp.cos(j * a_im * dt), m * jnp.sin(j * a_im * dt)

    p_r, p_i = zip(*[power(float(j)) for j in range(q + 1)])
    p_r, p_i = jnp.stack(p_r, 2), jnp.stack(p_i, 2)

    cp_r = c_re[:, :, None] * p_r[:, :, :q, None] - c_im[:, :, None] * p_i[:, :, :q, None]
    cp_i = c_re[:, :, None] * p_i[:, :, :q, None] + c_im[:, :, None] * p_r[:, :, :q, None]
    kd = (jnp.einsum("zgdcn,zgne->zgdce", cp_r, fb_r, precision=_HI)
          - jnp.einsum("zgdcn,zgne->zgdce", cp_i, fb_i, precision=_HI))
    lag = jnp.arange(q)[:, None] - jnp.arange(q)[None, :]
    dd = jnp.arange(q)[:, None, None]
    sel_f = (lag[None] == dd).astype(F32)
    sel_b = (-lag[None] == dd).astype(F32)
    toep = (jnp.einsum("dtu,gdce->gtcue", sel_f, kd[0], precision=_HI)
            + jnp.einsum("dtu,gdce->gtcue", sel_b, kd[1], precision=_HI)).reshape(ng, q * gs, q * gs)

    def inject(z, pw):
        pr, pi = p_r[z][:, pw], p_i[z][:, pw]
        re = pr[..., None] * fb_r[z][:, None] - pi[..., None] * fb_i[z][:, None]
        im = pr[..., None] * fb_i[z][:, None] + pi[..., None] * fb_r[z][:, None]
        to_rows = lambda v: v.transpose(0, 2, 1, 3).reshape(ng, ns, q * gs)
        return to_rows(re), to_rows(im)

    bf_r, bf_i = inject(0, jnp.arange(q - 1, -1, -1))
    bb_r, bb_i = inject(1, jnp.arange(q))
    tbm = jnp.concatenate([toep, bf_r, bb_r, bf_i, bb_i], axis=1).astype(BF16)

    def readout(z, pw):
        pr, pi = p_r[z][:, pw], p_i[z][:, pw]
        re = c_re[z][:, None] * pr[:, :, None] - c_im[z][:, None] * pi[:, :, None]
        im = c_re[z][:, None] * pi[:, :, None] + c_im[z][:, None] * pr[:, :, None]
        return re.reshape(ng, q * gs, ns), -im.reshape(ng, q * gs, ns)

    cf_r, cf_i = readout(0, jnp.arange(1, q + 1))
    cb_r, cb_i = readout(1, jnp.arange(q, 0, -1))
    cgm = jnp.concatenate([cf_r, cb_r, cf_i, cb_i], axis=2).astype(BF16)

    rows = []
    for lvl in range(7):
        pr, pi = power(float(q << lvl))
        rows += [v.reshape(ng // 2, 2 * ns) for v in (pr[0], pi[0], pr[1], pi[1])]
    rows += [jnp.zeros_like(rows[0])] * 4
    sc = jnp.stack(rows, axis=1)
    ones = jnp.ones((1, S5_NCHUNK), F32)
    return {"tb": tbm, "cg": cgm, "sc": sc, "dcol": d[:, None] * ones, "wgt": w_glu.T.astype(BF16),
            "bcol": b_glu[:, None] * ones}


def _prep_ssd(conv_w, conv_b, a_log, dt_bias, d):
    nh = SSD_HEADS
    a = (-jnp.exp(a_log)).reshape(1, 2 * nh)
    db = dt_bias.reshape(1, 2 * nh)
    head_of_lane = jnp.arange(SSD_WIDTH) // SSD_HEADDIM
    onehot = (jnp.arange(nh)[:, None] == head_of_lane[None, :]).astype(BF16)
    zero = jnp.zeros_like(onehot)
    i = jnp.arange(SSD_CHUNK)
    return {"cw": conv_w, "cb": conv_b.reshape(1, SSD_XBC), "a": a, "at": a.T, "db": db, "dbt": db.T,
            "dexp": jnp.repeat(d, SSD_HEADDIM).reshape(1, SSD_WIDTH),
            "ef": jnp.concatenate([onehot, zero], 0), "eb": jnp.concatenate([zero, onehot], 0),
            "tril": (i[:, None] >= i[None, :]).astype(BF16), "triu": (i[:, None] <= i[None, :]).astype(BF16)}


def kernel(x, norm_mix, w_in, hy_short_w, hy_short_b, hy_w1, hy_b1, hy_freq, hy_w2, hy_b2, hy_w3, hy_decay, hy_bias, s5_a_re, s5_a_im, s5_log_dt, s5_b_re, s5_b_im, s5_c_re, s5_c_im, s5_d, s5_w_glu, s5_b_glu, ssd_conv_w, ssd_conv_b, ssd_a_log, ssd_dt_bias, ssd_d, ssd_norm, p_a, p_b, p_c, w_out, norm_ffn, ffn_up, ffn_conv_w, ffn_conv_b, ffn_down, norm_final):
    bsz, seq, dm = x.shape
    assert (seq, dm) == (SEQ, D_MODEL)
    x2 = x.reshape(bsz * seq, dm)
    fc, fs = _dft_matrices()
    nf = norm_final.reshape(1, D_MODEL)
    for l in range(DEPTH):
        pw = _prep_proj(w_in[l])
        hw = _prep_hyena(hy_short_w[l], hy_short_b[l], hy_w1[l], hy_b1[l], hy_freq[l], hy_w2[l], hy_b2[l],
                         hy_w3[l], hy_decay[l], hy_bias[l])
        s5w = _prep_s5(s5_a_re[l], s5_a_im[l], s5_log_dt[l], s5_b_re[l], s5_b_im[l], s5_c_re[l], s5_c_im[l],
                       s5_d[l], s5_w_glu[l], s5_b_glu[l])
        sdw = _prep_ssd(ssd_conv_w[l], ssd_conv_b[l], ssd_a_log[l], ssd_dt_bias[l], ssd_d[l])
        mw = {"pa": p_a[l].astype(BF16), "pb": p_b[l].astype(BF16), "pc": p_c[l].astype(BF16),
              "wo": w_out[l].astype(BF16), "nc": ssd_norm[l].reshape(1, SSD_WIDTH),
              "nw": norm_ffn[l].reshape(1, D_MODEL)}
        fw = {"up": ffn_up[l].astype(BF16), "cw": ffn_conv_w[l], "cb": ffn_conv_b[l].reshape(1, -1),
              "down": ffn_down[l].astype(BF16)}

        hy_u, s5_u, z, xbc, dt, dtt, gates = _proj(x2, norm_mix[l].reshape(1, D_MODEL), pw)
        ya = _hyena(hy_u, hw, fc, fs)
        yb = _s5(s5_u, s5w)
        yc = _ssd(xbc, dt, dtt, sdw)
        x_mid, h2 = _merge(x2, ya, yb, yc, z, gates, mw)
        x2 = _ffn(h2, x_mid, fw, nf, final_norm=(l == DEPTH - 1))
    return x2.reshape(bsz, seq, dm)
```

```python
import functools
import math

import jax
import jax.numpy as jnp
from jax import lax
from jax.experimental import pallas as pl
from jax.experimental.pallas import tpu as pltpu

F32 = jnp.float32
BF16 = jnp.bfloat16

D_MODEL = 1024
SEQ = 2048
DEPTH = 2
EPS = 1e-6

HY_WIDTH = 512
HY_BANDS = 16
NFFT = 2 * SEQ

S5_WIDTH = 512
S5_GROUP = 16
S5_GROUPS = S5_WIDTH // S5_GROUP
S5_STATE = 64
S5_CHUNK = 16
S5_NCHUNK = SEQ // S5_CHUNK
S5_SLABS = S5_WIDTH // 128

SSD_WIDTH = 1024
SSD_HEADDIM = 64
SSD_HEADS = SSD_WIDTH // SSD_HEADDIM
SSD_GROUPS = 2
SSD_STATE = 128
SSD_BC = SSD_GROUPS * SSD_STATE
SSD_XBC = SSD_WIDTH + 2 * SSD_BC
SSD_CONV = 5
SSD_CHUNK = 128
SSD_NCHUNK = SEQ // SSD_CHUNK
HEADS_PER_GROUP = SSD_HEADS // SSD_GROUPS

FFN_HIDDEN = 2816
FFN_TILE = 256
FFN_NTILE = FFN_HIDDEN // FFN_TILE
FFN_HALO = 16

PROJ_TM = 512
HY_TILE = 256
HY_FTILE = 256
CONV_PAD = 8
VMEM_LIMIT = 56 * 1024 * 1024


def _cparams(sem):
    return pltpu.CompilerParams(dimension_semantics=sem, vmem_limit_bytes=VMEM_LIMIT)


def _const_spec(shape):
    nd = len(shape)
    return pl.BlockSpec(shape, lambda *_: (0,) * nd, pipeline_mode=pl.Buffered(1))


def _rms(x, w):
    return x * lax.rsqrt(jnp.mean(x * x, axis=-1, keepdims=True) + EPS) * w


def _dot(a, b):
    return jnp.dot(a, b, preferred_element_type=F32)


def _dot_nt(a, b):
    return lax.dot_general(a, b, (((1,), (1,)), ((), ())), preferred_element_type=F32)


def _dot_tn(a, b):
    return lax.dot_general(a, b, (((0,), (0,)), ((), ())), preferred_element_type=F32)


def _split3(a):
    a1 = a.astype(BF16)
    r1 = a - a1.astype(F32)
    a2 = r1.astype(BF16)
    a3 = (r1 - a2.astype(F32)).astype(BF16)
    return a1, a2, a3


def _shift_rows(x, j, row):
    n = x.shape[0]
    rolled = pltpu.roll(x, (-j) % n, 0)
    valid = (row >= -j) if j < 0 else (row < n - j)
    return jnp.where(valid, rolled, 0.0)


def _dft_kernel(fc_ref, fs_ref):
    rows = fc_ref.shape[0]
    r = lax.broadcasted_iota(jnp.int32, (rows, SEQ), 0) + pl.program_id(0) * rows
    c = lax.broadcasted_iota(jnp.int32, (rows, SEQ), 1)
    ang = ((r * c) & (NFFT - 1)).astype(F32) * (2.0 * math.pi / NFFT)
    fc_ref[...] = jnp.cos(ang).astype(BF16)
    fs_ref[...] = jnp.sin(ang).astype(BF16)


def _dft_matrices():
    rows = 256
    return pl.pallas_call(
        _dft_kernel,
        grid=(SEQ // rows,),
        out_specs=[pl.BlockSpec((rows, SEQ), lambda i: (i, 0))] * 2,
        out_shape=[jax.ShapeDtypeStruct((SEQ, SEQ), BF16)] * 2,
        compiler_params=_cparams(("arbitrary",)),
        name="dft_matrices",
    )()


def _proj_kernel(x_ref, xp_ref, xn_ref, nw_ref, why_ref, ws5_ref, wz_ref, wxbc_ref, wdt_ref, wdtt_ref, wg_ref,
                 hcw_ref, hcb_ref, scw_ref, scb_ref,
                 hy_ref, s5_ref, z_ref, xbc_ref, dt_ref, dtt_ref, g_ref, stage_a, stage_b):
    tm = x_ref.shape[0]
    tiles_per_seq = SEQ // tm
    pos = pl.program_id(0) % tiles_per_seq
    xp = jnp.where(pos == 0, 0.0, xp_ref[...])
    xn = jnp.where(pos == tiles_per_seq - 1, 0.0, xn_ref[...])
    hext = _rms(jnp.concatenate([xp, x_ref[...], xn], axis=0), nw_ref[...]).astype(BF16)
    ext = tm + 2 * FFN_HALO
    h = hext[FFN_HALO:FFN_HALO + tm]

    def conv(w_ref, cw_ref, cb_ref, cols, stage):
        stage[...] = _dot(hext, w_ref[:, cols])
        w = cw_ref[:, cols]
        half = w.shape[0] // 2
        acc = cb_ref[:, cols]
        for k in range(w.shape[0]):
            acc = acc + w[k:k + 1] * stage[pl.ds(FFN_HALO + k - half, tm), :]
        return acc

    ctile = stage_a.shape[1]
    nconv = 3 * HY_WIDTH // ctile
    for j in range(2 * nconv):
        cols = slice(ctile * (j % nconv), ctile * (j % nconv + 1))
        stage = stage_a if j % 2 == 0 else stage_b
        if j < nconv:
            hy_ref[:, cols] = conv(why_ref, hcw_ref, hcb_ref, cols, stage).astype(BF16)
        else:
            xbc = conv(wxbc_ref, scw_ref, scb_ref, cols, stage)
            xbc_ref[:, cols] = (xbc * jax.nn.sigmoid(xbc)).astype(BF16)
        gcols = slice(ctile * j, ctile * (j + 1))
        g_ref[:, gcols] = jax.nn.sigmoid(_dot(h, wg_ref[:, gcols])).astype(BF16)
    u = _dot(h, ws5_ref[...])
    for j in range(S5_SLABS):
        s5_ref[j] = u[:, 128 * j:128 * (j + 1)]
    z = _dot(h, wz_ref[...])
    z_ref[...] = (z * jax.nn.sigmoid(z)).astype(BF16)
    dt_ref[...] = _dot(h, wdt_ref[...])
    dtt_ref[...] = _dot_nt(wdtt_ref[...], h)


def _proj(x2, nw, w):
    t = x2.shape[0]
    tm = PROJ_TM
    hb = tm // FFN_HALO
    last = t // FFN_HALO - 1
    row = lambda n: pl.BlockSpec((tm, n), lambda i: (i, 0))
    names = ("hy", "s5", "z", "xbc", "dt", "dtt", "gate", "hcw", "hcb", "scw", "scb")
    return pl.pallas_call(
        _proj_kernel,
        grid=(t // tm,),
        in_specs=[row(D_MODEL),
                  pl.BlockSpec((FFN_HALO, D_MODEL), lambda i: (jnp.maximum(i * hb - 1, 0), 0)),
                  pl.BlockSpec((FFN_HALO, D_MODEL), lambda i: (jnp.minimum((i + 1) * hb, last), 0)),
                  _const_spec((1, D_MODEL))] + [_const_spec(w[n].shape) for n in names],
        out_specs=[row(3 * HY_WIDTH),
                   pl.BlockSpec((S5_SLABS, tm, 128), lambda i: (0, i, 0)),
                   row(SSD_WIDTH), row(SSD_XBC), row(2 * SSD_HEADS),
                   pl.BlockSpec((2 * SSD_HEADS, tm), lambda i: (0, i)),
                   row(3 * D_MODEL)],
        out_shape=[jax.ShapeDtypeStruct((t, 3 * HY_WIDTH), BF16),
                   jax.ShapeDtypeStruct((S5_SLABS, t, 128), F32),
                   jax.ShapeDtypeStruct((t, SSD_WIDTH), BF16),
                   jax.ShapeDtypeStruct((t, SSD_XBC), BF16),
                   jax.ShapeDtypeStruct((t, 2 * SSD_HEADS), F32),
                   jax.ShapeDtypeStruct((2 * SSD_HEADS, t), F32),
                   jax.ShapeDtypeStruct((t, 3 * D_MODEL), BF16)],
        scratch_shapes=[pltpu.VMEM((tm + 2 * FFN_HALO, 256), F32)] * 2,
        compiler_params=_cparams(("arbitrary",)),
        name="in_proj",
    )(x2, x2, x2, nw, *[w[n] for n in names])


def _hyena_kernel(x0_ref, x1_ref, v_ref, fc_ref, fs_ref, kre_ref, kim_ref, knyq_ref, bias_ref, o_ref):
    row = lax.broadcasted_iota(jnp.int32, (SEQ, HY_TILE), 0)
    s = v_ref[...].astype(F32) * x1_ref[...].astype(F32)
    sb = s.astype(BF16)
    y = None
    for j in range(SEQ // HY_FTILE):
        fr = slice(HY_FTILE * j, HY_FTILE * (j + 1))
        sre = _dot(fc_ref[fr, :], sb)
        sim = _dot(fs_ref[fr, :], sb)
        kre = kre_ref[fr, :]
        kim = kim_ref[fr, :]
        yre = (sre * kre + sim * kim).astype(BF16)
        yim = (sim * kre - sre * kim).astype(BF16)
        part = _dot(fc_ref[:, fr], yre) + _dot(fs_ref[:, fr], yim)
        y = part if y is None else y + part
    alt = jnp.where((row & 1) == 0, 1.0, -1.0)
    snyq = jnp.sum(s * alt, axis=0, keepdims=True)
    y = y + alt * (snyq * knyq_ref[...])
    o_ref[...] = (x0_ref[...].astype(F32) * (y + bias_ref[...] * s)).astype(o_ref.dtype)


def _hyena(hy_u, hw, fc, fs):
    t = hy_u.shape[0]
    nb = t // SEQ
    nj = HY_WIDTH // HY_TILE
    part = lambda p: pl.BlockSpec((SEQ, HY_TILE), lambda j, b, p=p: (b, p * nj + j))
    ctile = lambda r: pl.BlockSpec((r, HY_TILE), lambda j, b: (0, j))
    return pl.pallas_call(
        _hyena_kernel,
        grid=(nj, nb),
        in_specs=[part(0), part(1), part(2),
                  _const_spec((SEQ, SEQ)), _const_spec((SEQ, SEQ)),
                  ctile(SEQ), ctile(SEQ), ctile(1), ctile(1)],
        out_specs=pl.BlockSpec((SEQ, HY_TILE), lambda j, b: (b, j)),
        out_shape=jax.ShapeDtypeStruct((t, HY_WIDTH), BF16),
        compiler_params=_cparams(("arbitrary", "arbitrary")),
        name="hyena",
    )(hy_u, hy_u, hy_u, fc, fs, hw["kre"], hw["kim"], hw["knyq"], hw["bias"])


def _s5_kernel(u_ref, tb_ref, cg_ref, sc_ref, dcol_ref, wgt_ref, bcol_ref, o_ref, ut_scr, yt_scr):
    nk = S5_NCHUNK
    for t in range(S5_CHUNK):
        for j in range(S5_SLABS):
            blk = u_ref[j, pl.ds(t, nk, stride=S5_CHUNK), :]
            ut_scr[t, 128 * j:128 * (j + 1), :] = blk.T

    half = S5_STATE
    row = lax.broadcasted_iota(jnp.int32, (nk, 2 * half), 0)

    def from_prev(x, s):
        if s % 8 == 0:
            return jnp.concatenate([jnp.zeros((s, 2 * half), F32), x[:nk - s]], axis=0)
        return jnp.where(row >= s, pltpu.roll(x, s, 0), 0.0)

    def from_next(x, s):
        if s % 8 == 0:
            return jnp.concatenate([x[s:], jnp.zeros((s, 2 * half), F32)], axis=0)
        return jnp.where(row < nk - s, pltpu.roll(x, nk - s, 0), 0.0)

    def group_pair(p, carry):
        rs = []
        for i in range(2):
            r0 = pl.multiple_of((2 * p + i) * S5_GROUP, S5_GROUP)
            ucol = ut_scr[:, pl.ds(r0, S5_GROUP), :].reshape(S5_CHUNK * S5_GROUP, nk).astype(BF16)
            rs.append(_dot(tb_ref[2 * p + i], ucol))

        def states(lo):
            return jnp.concatenate([rs[0][lo:lo + half], rs[1][lo:lo + half]], axis=0).T

        fr, br, fi, bi = states(256), states(256 + half), states(256 + 2 * half), states(256 + 3 * half)
        sc = sc_ref[p]
        for lvl in range(7):
            s = 1 << lvl
            arf, aif, arb, aib = (sc[4 * lvl + i:4 * lvl + i + 1] for i in range(4))
            pfr, pfi = from_prev(fr, s), from_prev(fi, s)
            pbr, pbi = from_next(br, s), from_next(bi, s)
            fr, fi = fr + arf * pfr - aif * pfi, fi + arf * pfi + aif * pfr
            br, bi = br + arb * pbr - aib * pbi, bi + arb * pbi + aib * pbr
        carried = [from_prev(fr, 1).T, from_next(br, 1).T, from_prev(fi, 1).T, from_next(bi, 1).T]
        for i in range(2):
            r0 = pl.multiple_of((2 * p + i) * S5_GROUP, S5_GROUP)
            hs = jnp.concatenate([c[half * i:half * (i + 1)] for c in carried], axis=0).astype(BF16)
            z = rs[i][0:256] + _dot(cg_ref[2 * p + i], hs)
            yt_scr[:, pl.ds(r0, S5_GROUP), :] = z.reshape(S5_CHUNK, S5_GROUP, nk)
        return carry

    lax.fori_loop(0, S5_GROUPS // 2, group_pair, 0)

    for t in range(S5_CHUNK):
        gl = jax.nn.gelu(yt_scr[t] + dcol_ref[...] * ut_scr[t])
        z2 = _dot(wgt_ref[...], gl.astype(BF16)) + bcol_ref[...]
        yb = gl * jax.nn.sigmoid(z2)
        for j in range(S5_SLABS):
            o_ref[j, pl.ds(t, nk, stride=S5_CHUNK), :] = yb[128 * j:128 * (j + 1), :].T


def _s5(s5_u, sw):
    t = s5_u.shape[1]
    nb = t // SEQ
    slab = pl.BlockSpec((S5_SLABS, SEQ, 128), lambda b: (0, b, 0))
    return pl.pallas_call(
        _s5_kernel,
        grid=(nb,),
        in_specs=[slab, _const_spec(sw["tb"].shape), _const_spec(sw["cg"].shape), _const_spec(sw["sc"].shape),
                  _const_spec(sw["dcol"].shape), _const_spec(sw["wgt"].shape), _const_spec(sw["bcol"].shape)],
        out_specs=slab,
        out_shape=jax.ShapeDtypeStruct((S5_SLABS, t, 128), F32),
        scratch_shapes=[pltpu.VMEM((S5_CHUNK, S5_WIDTH, S5_NCHUNK), F32),
                        pltpu.VMEM((S5_CHUNK, S5_WIDTH, S5_NCHUNK), F32)],
        compiler_params=_cparams(("arbitrary",)),
        name="s5",
    )(s5_u, sw["tb"], sw["cg"], sw["sc"], sw["dcol"], sw["wgt"], sw["bcol"])


def _ssd_kernel(xs_scr, bc_scr, dt_ref, dtt_ref, a_ref, at_ref, db_ref, dbt_ref, dexp_ref,
                ef_ref, eb_ref, tril_ref, triu_ref, o_ref,
                y_scr, carry_scr, dt_scr, a_scr, dtt_scr, att_scr, cs_scr):
    q = SSD_CHUNK
    nh = SSD_HEADS

    dtv = jax.nn.softplus(dt_ref[...] + db_ref[...])
    dt_scr[...] = dtv
    a_scr[...] = dtv * a_ref[...]
    dtvt = jax.nn.softplus(dtt_ref[...] + dbt_ref[...])
    avt = dtvt * at_ref[...]
    for c in range(SSD_NCHUNK):
        dtt_scr[c] = dtvt[:, q * c:q * (c + 1)]
        att_scr[c] = avt[:, q * c:q * (c + 1)]

    ri = lax.broadcasted_iota(jnp.int32, (q, q), 0)
    ci = lax.broadcasted_iota(jnp.int32, (q, q), 1)
    causal = ri >= ci
    anti = ci >= ri
    col_is_f = lax.broadcasted_iota(jnp.int32, (q, 2 * nh), 1) < nh
    row_is_f = lax.broadcasted_iota(jnp.int32, (2 * nh, q), 0) < nh
    lane_lo = lax.broadcasted_iota(jnp.int32, (q, 128), 1) < SSD_HEADDIM
    tril = tril_ref[...]
    triu = triu_ref[...]

    def split2(a):
        a1 = a.astype(BF16)
        return a1, (a - a1.astype(F32)).astype(BF16)

    def col_cumsum(a_c):
        parts = split2(a_c)
        lo = sum(_dot(tril, p) for p in parts)
        hi = sum(_dot(triu, p) for p in parts)
        return jnp.where(col_is_f, lo, hi)

    def row_cumsum(at_c):
        parts = split2(at_c)
        lo = sum(_dot(p, triu) for p in parts)
        hi = sum(_dot(p, tril) for p in parts)
        return jnp.where(row_is_f, lo, hi)

    def expand(v, e_ref):
        return sum(_dot(p, e_ref[...]) for p in split2(v))

    def inter_chunk(c, cs, e_ref):
        r0 = pl.multiple_of(c * q, q)
        total = jnp.where(col_is_f[0:1], cs[q - 1:q], cs[0:1])
        w_state = _dot((dt_scr[pl.ds(r0, q), :] * jnp.exp(total - cs)).astype(BF16), e_ref[...])
        w_in = _dot(jnp.exp(cs).astype(BF16), e_ref[...])
        decay = expand(jnp.broadcast_to(jnp.exp(total), (8, 2 * nh)), e_ref)[0:1]
        xdd = (xs_scr[pl.ds(r0, q), :] * w_state).astype(BF16)
        outs = []
        for g in range(SSD_GROUPS):
            bg = bc_scr[pl.ds(r0, q), SSD_STATE * g:SSD_STATE * (g + 1)]
            cg = bc_scr[pl.ds(r0, q), SSD_BC + SSD_STATE * g:SSD_BC + SSD_STATE * (g + 1)]
            lanes = slice(512 * g, 512 * (g + 1))
            prev = carry_scr[g]
            outs.append(_dot(cg, prev.astype(BF16)) * w_in[:, lanes])
            carry_scr[g] = prev * decay[:, lanes] + _dot_tn(bg, xdd[:, lanes])
        return r0, jnp.concatenate(outs, axis=1)

    def forward_chunk(c, carry):
        r0 = pl.multiple_of(c * q, q)
        cs = col_cumsum(a_scr[pl.ds(r0, q), :])
        cs_scr[pl.ds(r0, q), :] = cs
        rs = row_cumsum(att_scr[c])
        dtt_c = dtt_scr[c]
        xs_b = xs_scr[pl.ds(r0, q), :]
        pairs = []
        for g in range(SSD_GROUPS):
            bg = bc_scr[pl.ds(r0, q), SSD_STATE * g:SSD_STATE * (g + 1)]
            cg = bc_scr[pl.ds(r0, q), SSD_BC + SSD_STATE * g:SSD_BC + SSD_STATE * (g + 1)]
            scores = _dot_nt(cg, bg)
            for j in range(HEADS_PER_GROUP // 2):
                ms = []
                for h in (HEADS_PER_GROUP * g + 2 * j, HEADS_PER_GROUP * g + 2 * j + 1):
                    hb = nh + h
                    lf = jnp.where(causal, jnp.exp(cs[:, h:h + 1] - rs[h:h + 1, :]), 0.0) * dtt_c[h:h + 1, :]
                    lb = jnp.where(anti, jnp.exp(cs[:, hb:hb + 1] - rs[hb:hb + 1, :]), 0.0) * dtt_c[hb:hb + 1, :]
                    ms.append((scores * (lf + lb)).astype(BF16))
                blk = HEADS_PER_GROUP // 2 * g + j
                xp = xs_b[:, 128 * blk:128 * (blk + 1)]
                rhs = jnp.concatenate([jnp.where(lane_lo, xp, 0), jnp.where(lane_lo, 0, xp)], axis=0)
                pairs.append(_dot(jnp.concatenate(ms, axis=1), rhs))
        _, y_off = inter_chunk(c, cs, ef_ref)
        y_scr[pl.ds(r0, q), :] = jnp.concatenate(pairs, axis=1) + y_off
        return carry

    def backward_chunk(i, carry):
        c = SSD_NCHUNK - 1 - i
        r0, y_off = inter_chunk(c, cs_scr[pl.ds(pl.multiple_of(c * q, q), q), :], eb_ref)
        y_scr[pl.ds(r0, q), :] += y_off
        return carry

    carry_scr[...] = jnp.zeros_like(carry_scr)
    lax.fori_loop(0, SSD_NCHUNK, forward_chunk, 0, unroll=2)
    carry_scr[...] = jnp.zeros_like(carry_scr)
    lax.fori_loop(0, SSD_NCHUNK, backward_chunk, 0, unroll=2)

    rows = 256
    for i in range(SEQ // rows):
        sl = slice(rows * i, rows * (i + 1))
        o_ref[sl, :] = (y_scr[sl, :] + dexp_ref[...] * xs_scr[sl, :]).astype(o_ref.dtype)


def _ssd(xbc, dt, dtt, sw):
    t = xbc.shape[0]
    nb = t // SEQ
    row = lambda n: pl.BlockSpec((SEQ, n), lambda b: (b, 0))
    names = ("a", "at", "db", "dbt", "dexp", "ef", "eb", "tril", "triu")
    return pl.pallas_call(
        _ssd_kernel,
        grid=(nb,),
        in_specs=[row(SSD_WIDTH),
                  pl.BlockSpec((SEQ, 2 * SSD_BC), lambda b: (b, SSD_WIDTH // (2 * SSD_BC))),
                  row(2 * SSD_HEADS),
                  pl.BlockSpec((2 * SSD_HEADS, SEQ), lambda b: (0, b))]
                 + [_const_spec(sw[n].shape) for n in names],
        out_specs=row(SSD_WIDTH),
        out_shape=jax.ShapeDtypeStruct((t, SSD_WIDTH), BF16),
        scratch_shapes=[pltpu.VMEM((SEQ, SSD_WIDTH), F32),
                        pltpu.VMEM((SSD_GROUPS, SSD_STATE, SSD_WIDTH // SSD_GROUPS), F32),
                        pltpu.VMEM((SEQ, 2 * SSD_HEADS), F32),
                        pltpu.VMEM((SEQ, 2 * SSD_HEADS), F32),
                        pltpu.VMEM((SSD_NCHUNK, 2 * SSD_HEADS, SSD_CHUNK), F32),
                        pltpu.VMEM((SSD_NCHUNK, 2 * SSD_HEADS, SSD_CHUNK), F32),
                        pltpu.VMEM((SEQ, 2 * SSD_HEADS), F32)],
        compiler_params=_cparams(("arbitrary",)),
        name="ssd",
    )(xbc, xbc, dt, dtt, *[sw[n] for n in names])


def _merge_kernel(x_ref, ya_ref, yb_ref, yc_ref, z_ref, g_ref, pa_ref, pb_ref, pc_ref, wo_ref, nc_ref, nw_ref,
                  xo_ref, h_ref):
    gate = g_ref[...].astype(F32)
    yb = jnp.concatenate([yb_ref[j] for j in range(S5_SLABS)], axis=1).astype(BF16)
    yc = _rms(yc_ref[...].astype(F32) * z_ref[...].astype(F32), nc_ref[...]).astype(BF16)
    merged = (gate[:, 0:D_MODEL] * _dot(ya_ref[...], pa_ref[...])
              + gate[:, D_MODEL:2 * D_MODEL] * _dot(yb, pb_ref[...])
              + gate[:, 2 * D_MODEL:] * _dot(yc, pc_ref[...]))
    xn = x_ref[...] + _dot(merged.astype(BF16), wo_ref[...])
    xo_ref[...] = xn
    h_ref[...] = _rms(xn, nw_ref[...]).astype(BF16)


def _merge(x2, ya, yb, yc, z, gates, mw):
    t = x2.shape[0]
    tm = PROJ_TM
    row = lambda n: pl.BlockSpec((tm, n), lambda i: (i, 0))
    return pl.pallas_call(
        _merge_kernel,
        grid=(t // tm,),
        in_specs=[row(D_MODEL), row(HY_WIDTH), pl.BlockSpec((S5_SLABS, tm, 128), lambda i: (0, i, 0)),
                  row(SSD_WIDTH), row(SSD_WIDTH), row(3 * D_MODEL),
                  _const_spec(mw["pa"].shape), _const_spec(mw["pb"].shape), _const_spec(mw["pc"].shape),
                  _const_spec(mw["wo"].shape), _const_spec((1, SSD_WIDTH)), _const_spec((1, D_MODEL))],
        out_specs=[row(D_MODEL), row(D_MODEL)],
        out_shape=[jax.ShapeDtypeStruct((t, D_MODEL), F32), jax.ShapeDtypeStruct((t, D_MODEL), BF16)],
        compiler_params=_cparams(("arbitrary",)),
        name="merge",
    )(x2, ya, yb, yc, z, gates, mw["pa"], mw["pb"], mw["pc"], mw["wo"], mw["nc"], mw["nw"])


def _ffn_kernel(h_ref, hp_ref, hn_ref, x_ref, wu_ref, cw_ref, cb_ref, wd_ref, nf_ref, o_ref, *, final_norm):
    tm = h_ref.shape[0]
    tiles_per_seq = SEQ // tm
    pos = pl.program_id(0) % tiles_per_seq
    prev = jnp.where(pos == 0, jnp.zeros_like(hp_ref[...]), hp_ref[...])
    nxt = jnp.where(pos == tiles_per_seq - 1, jnp.zeros_like(hn_ref[...]), hn_ref[...])
    hext = jnp.concatenate([prev, h_ref[...], nxt], axis=0)
    ext = tm + 2 * FFN_HALO

    def conv(cols):
        u = _dot(hext, wu_ref[:, cols])
        w = cw_ref[:, cols]
        y = w[0:1] * pltpu.roll(u, 1, 0) + w[1:2] * u + w[2:3] * pltpu.roll(u, ext - 1, 0)
        return y[FFN_HALO:FFN_HALO + tm] + cb_ref[:, cols]

    acts = []
    for j in range(FFN_NTILE):
        gate = conv(slice(FFN_TILE * j, FFN_TILE * (j + 1)))
        val = conv(slice(FFN_HIDDEN + FFN_TILE * j, FFN_HIDDEN + FFN_TILE * (j + 1)))
        acts.append((gate * jax.nn.sigmoid(gate) * val).astype(BF16))
    y = x_ref[...] + _dot(jnp.concatenate(acts, axis=1), wd_ref[...])
    o_ref[...] = _rms(y, nf_ref[...]) if final_norm else y


def _ffn(h2, x2, fw, nf, final_norm):
    t = x2.shape[0]
    tm = PROJ_TM
    hb = tm // FFN_HALO
    last = t // FFN_HALO - 1
    row = lambda n: pl.BlockSpec((tm, n), lambda i: (i, 0))
    return pl.pallas_call(
        functools.partial(_ffn_kernel, final_norm=final_norm),
        grid=(t // tm,),
        in_specs=[row(D_MODEL),
                  pl.BlockSpec((FFN_HALO, D_MODEL), lambda i: (jnp.maximum(i * hb - 1, 0), 0)),
                  pl.BlockSpec((FFN_HALO, D_MODEL), lambda i: (jnp.minimum((i + 1) * hb, last), 0)),
                  row(D_MODEL),
                  _const_spec(fw["up"].shape), _const_spec(fw["cw"].shape), _const_spec(fw["cb"].shape),
                  _const_spec(fw["down"].shape), _const_spec((1, D_MODEL))],
        out_specs=row(D_MODEL),
        out_shape=jax.ShapeDtypeStruct((t, D_MODEL), F32),
        compiler_params=_cparams(("arbitrary",)),
        name="conv_ffn",
    )(h2, h2, h2, x2, fw["up"], fw["cw"], fw["cb"], fw["down"], nf)


_HI = lax.Precision.HIGHEST


def _prep_proj(w_in, hy_short_w, hy_short_b, ssd_conv_w, ssd_conv_b):
    o = [0, 3 * HY_WIDTH, 3 * HY_WIDTH + S5_WIDTH]
    o.append(o[-1] + SSD_WIDTH)
    o.append(o[-1] + SSD_XBC)
    o.append(o[-1] + 2 * SSD_HEADS)
    wb = w_in.astype(BF16)
    return {"hy": wb[:, o[0]:o[1]], "s5": wb[:, o[1]:o[2]], "z": wb[:, o[2]:o[3]], "xbc": wb[:, o[3]:o[4]],
            "dt": wb[:, o[4]:o[5]], "dtt": wb[:, o[4]:o[5]].T, "gate": wb[:, o[5]:],
            "hcw": hy_short_w, "hcb": hy_short_b.reshape(1, -1),
            "scw": ssd_conv_w, "scb": ssd_conv_b.reshape(1, -1)}


def _prep_hyena(w1, b1, freq, w2, b2, w3, decay, bias):
    pos = jnp.arange(SEQ, dtype=F32)
    tt = (pos / max(SEQ - 1, 1))[:, None]
    bands = jnp.linspace(1e-4, HY_BANDS - 1, HY_BANDS, dtype=F32)
    ang = (2.0 * math.pi / SEQ) * pos[:, None] * bands[None, :]
    zz = jnp.concatenate([tt, jnp.cos(ang), -jnp.sin(ang)], axis=-1)
    h = jnp.sin(freq[0] * (jnp.dot(zz, w1, precision=_HI) + b1))
    h = jnp.sin(freq[1] * (jnp.dot(h, w2, precision=_HI) + b2))
    h = jnp.dot(h, w3, precision=_HI) * jnp.exp(-tt * jnp.abs(decay))
    h = h.reshape(SEQ, 2, HY_WIDTH)
    k = jnp.concatenate([h[:, 0], jnp.zeros((1, HY_WIDTH), F32), h[:0:-1, 1]], axis=0)
    k = k / jnp.sum(jnp.abs(k), axis=0, keepdims=True)
    kf = jnp.fft.rfft(k, axis=0)
    wgt = jnp.where(jnp.arange(SEQ) == 0, 1.0, 2.0)[:, None] / NFFT
    return {"kre": jnp.real(kf[:SEQ]) * wgt, "kim": jnp.imag(kf[:SEQ]) * wgt,
            "knyq": jnp.real(kf[SEQ:]) / NFFT, "bias": bias.reshape(1, HY_WIDTH)}


def _prep_s5(a_re, a_im, log_dt, b_re, b_im, c_re, c_im, d, w_glu, b_glu):
    q, gs, ns, ng = S5_CHUNK, S5_GROUP, S5_STATE, S5_GROUPS
    dt = jnp.exp(log_dt)[:, :, None]
    mag = jnp.exp(a_re * dt)
    ab_r, ab_i = mag * jnp.cos(a_im * dt), mag * jnp.sin(a_im * dt)
    den = a_re * a_re + a_im * a_im
    f_r = ((ab_r - 1.0) * a_re + ab_i * a_im) / den
    f_i = (ab_i * a_re - (ab_r - 1.0) * a_im) / den
    fb_r = f_r[..., None] * b_re - f_i[..., None] * b_im
    fb_i = f_r[..., None] * b_im + f_i[..., None] * b_re

    def power(j):
        m = jnp.exp(j * a_re * dt)
        return m * jnp.cos(j * a_im * dt), m * jnp.sin(j * a_im * dt)

    p_r, p_i = zip(*[power(float(j)) for j in range(q + 1)])
    p_r, p_i = jnp.stack(p_r, 2), jnp.stack(p_i, 2)

    cp_r = c_re[:, :, None] * p_r[:, :, :q, None] - c_im[:, :, None] * p_i[:, :, :q, None]
    cp_i = c_re[:, :, None] * p_i[:, :, :q, None] + c_im[:, :, None] * p_r[:, :, :q, None]
    kd = (jnp.einsum("zgdcn,zgne->zgdce", cp_r, fb_r, precision=_HI)
          - jnp.einsum("zgdcn,zgne->zgdce", cp_i, fb_i, precision=_HI))
    lag = jnp.arange(q)[:, None] - jnp.arange(q)[None, :]
    dd = jnp.arange(q)[:, None, None]
    sel_f = (lag[None] == dd).astype(F32)
    sel_b = (-lag[None] == dd).astype(F32)
    toep = (jnp.einsum("dtu,gdce->gtcue", sel_f, kd[0], precision=_HI)
            + jnp.einsum("dtu,gdce->gtcue", sel_b, kd[1], precision=_HI)).reshape(ng, q * gs, q * gs)

    def inject(z, pw):
        pr, pi = p_r[z][:, pw], p_i[z][:, pw]
        re = pr[..., None] * fb_r[z][:, None] - pi[..., None] * fb_i[z][:, None]
        im = pr[..., None] * fb_i[z][:, None] + pi[..., None] * fb_r[z][:, None]
        to_rows = lambda v: v.transpose(0, 2, 1, 3).reshape(ng, ns, q * gs)
        return to_rows(re), to_rows(im)

    bf_r, bf_i = inject(0, jnp.arange(q - 1, -1, -1))
    bb_r, bb_i = inject(1, jnp.arange(q))
    tbm = jnp.concatenate([toep, bf_r, bb_r, bf_i, bb_i], axis=1).astype(BF16)

    def readout(z, pw):
        pr, pi = p_r[z][:, pw], p_i[z][:, pw]
        re = c_re[z][:, None] * pr[:, :, None] - c_im[z][:, None] * pi[:, :, None]
        im = c_re[z][:, None] * pi[:, :, None] + c_im[z][:, None] * pr[:, :, None]
        return re.reshape(ng, q * gs, ns), -im.reshape(ng, q * gs, ns)

    cf_r, cf_i = readout(0, jnp.arange(1, q + 1))
    cb_r, cb_i = readout(1, jnp.arange(q, 0, -1))
    cgm = jnp.concatenate([cf_r, cb_r, cf_i, cb_i], axis=2).astype(BF16)

    rows = []
    for lvl in range(7):
        pr, pi = power(float(q << lvl))
        rows += [v.reshape(ng // 2, 2 * ns) for v in (pr[0], pi[0], pr[1], pi[1])]
    rows += [jnp.zeros_like(rows[0])] * 4
    sc = jnp.stack(rows, axis=1)
    ones = jnp.ones((1, S5_NCHUNK), F32)
    return {"tb": tbm, "cg": cgm, "sc": sc, "dcol": d[:, None] * ones, "wgt": w_glu.T.astype(BF16),
            "bcol": b_glu[:, None] * ones}


def _prep_ssd(a_log, dt_bias, d):
    nh = SSD_HEADS
    a = (-jnp.exp(a_log)).reshape(1, 2 * nh)
    db = dt_bias.reshape(1, 2 * nh)
    head_of_lane = jnp.arange(SSD_WIDTH) // SSD_HEADDIM
    onehot = (jnp.arange(nh)[:, None] == head_of_lane[None, :]).astype(BF16)
    zero = jnp.zeros_like(onehot)
    i = jnp.arange(SSD_CHUNK)
    return {"a": a, "at": a.T, "db": db, "dbt": db.T,
            "dexp": jnp.repeat(d, SSD_HEADDIM).reshape(1, SSD_WIDTH),
            "ef": jnp.concatenate([onehot, zero], 0), "eb": jnp.concatenate([zero, onehot], 0),
            "tril": (i[:, None] >= i[None, :]).astype(BF16), "triu": (i[:, None] <= i[None, :]).astype(BF16)}


def kernel(x, norm_mix, w_in, hy_short_w, hy_short_b, hy_w1, hy_b1, hy_freq, hy_w2, hy_b2, hy_w3, hy_decay, hy_bias, s5_a_re, s5_a_im, s5_log_dt, s5_b_re, s5_b_im, s5_c_re, s5_c_im, s5_d, s5_w_glu, s5_b_glu, ssd_conv_w, ssd_conv_b, ssd_a_log, ssd_dt_bias, ssd_d, ssd_norm, p_a, p_b, p_c, w_out, norm_ffn, ffn_up, ffn_conv_w, ffn_conv_b, ffn_down, norm_final):
    bsz, seq, dm = x.shape
    assert (seq, dm) == (SEQ, D_MODEL)
    x2 = x.reshape(bsz * seq, dm)
    fc, fs = _dft_matrices()
    nf = norm_final.reshape(1, D_MODEL)
    for l in range(DEPTH):
        pw = _prep_proj(w_in[l], hy_short_w[l], hy_short_b[l], ssd_conv_w[l], ssd_conv_b[l])
        hw = _prep_hyena(hy_w1[l], hy_b1[l], hy_freq[l], hy_w2[l], hy_b2[l], hy_w3[l], hy_decay[l], hy_bias[l])
        s5w = _prep_s5(s5_a_re[l], s5_a_im[l], s5_log_dt[l], s5_b_re[l], s5_b_im[l], s5_c_re[l], s5_c_im[l],
                       s5_d[l], s5_w_glu[l], s5_b_glu[l])
        sdw = _prep_ssd(ssd_a_log[l], ssd_dt_bias[l], ssd_d[l])
        mw = {"pa": p_a[l].astype(BF16), "pb": p_b[l].astype(BF16), "pc": p_c[l].astype(BF16),
              "wo": w_out[l].astype(BF16), "nc": ssd_norm[l].reshape(1, SSD_WIDTH),
              "nw": norm_ffn[l].reshape(1, D_MODEL)}
        fw = {"up": ffn_up[l].astype(BF16), "cw": ffn_conv_w[l], "cb": ffn_conv_b[l].reshape(1, -1),
              "down": ffn_down[l].astype(BF16)}

        hy_u, s5_u, z, xbc, dt, dtt, gates = _proj(x2, norm_mix[l].reshape(1, D_MODEL), pw)
        ya = _hyena(hy_u, hw, fc, fs)
        yb = _s5(s5_u, s5w)
        yc = _ssd(xbc, dt, dtt, sdw)
        x_mid, h2 = _merge(x2, ya, yb, yc, z, gates, mw)
        x2 = _ffn(h2, x_mid, fw, nf, final_norm=(l == DEPTH - 1))
    return x2.reshape(bsz, seq, dm)
```

```python
import functools
import math

import jax
import jax.numpy as jnp
from jax import lax
from jax.experimental import pallas as pl
from jax.experimental.pallas import tpu as pltpu

F32 = jnp.float32
BF16 = jnp.bfloat16

D_MODEL = 1024
SEQ = 2048
DEPTH = 2
EPS = 1e-6

HY_WIDTH = 512
HY_BANDS = 16
NFFT = 2 * SEQ

S5_WIDTH = 512
S5_GROUP = 16
S5_GROUPS = S5_WIDTH // S5_GROUP
S5_STATE = 64
S5_CHUNK = 16
S5_NCHUNK = SEQ // S5_CHUNK
S5_SLABS = S5_WIDTH // 128
S5_GB = 8

SSD_WIDTH = 1024
SSD_HEADDIM = 64
SSD_HEADS = SSD_WIDTH // SSD_HEADDIM
SSD_GROUPS = 2
SSD_STATE = 128
SSD_BC = SSD_GROUPS * SSD_STATE
SSD_XBC = SSD_WIDTH + 2 * SSD_BC
SSD_CONV = 5
SSD_CHUNK = 128
SSD_NCHUNK = SEQ // SSD_CHUNK
HEADS_PER_GROUP = SSD_HEADS // SSD_GROUPS

FFN_HIDDEN = 2816
FFN_TILE = 256
FFN_NTILE = FFN_HIDDEN // FFN_TILE
FFN_HALO = 16

PROJ_TM = 512
HY_TILE = 256
HY_FTILE = 256
CONV_PAD = 8
VMEM_LIMIT = 56 * 1024 * 1024


def _cparams(sem):
    return pltpu.CompilerParams(dimension_semantics=sem, vmem_limit_bytes=VMEM_LIMIT)


def _const_spec(shape):
    nd = len(shape)
    return pl.BlockSpec(shape, lambda *_: (0,) * nd, pipeline_mode=pl.Buffered(1))


def _rms(x, w):
    return x * lax.rsqrt(jnp.mean(x * x, axis=-1, keepdims=True) + EPS) * w


def _dot(a, b):
    return jnp.dot(a, b, preferred_element_type=F32)


def _dot_nt(a, b):
    return lax.dot_general(a, b, (((1,), (1,)), ((), ())), preferred_element_type=F32)


def _dot_tn(a, b):
    return lax.dot_general(a, b, (((0,), (0,)), ((), ())), preferred_element_type=F32)


def _split3(a):
    a1 = a.astype(BF16)
    r1 = a - a1.astype(F32)
    a2 = r1.astype(BF16)
    a3 = (r1 - a2.astype(F32)).astype(BF16)
    return a1, a2, a3


def _shift_rows(x, j, row):
    n = x.shape[0]
    rolled = pltpu.roll(x, (-j) % n, 0)
    valid = (row >= -j) if j < 0 else (row < n - j)
    return jnp.where(valid, rolled, 0.0)


def _dft_kernel(fc_ref, fs_ref):
    rows = fc_ref.shape[0]
    r = lax.broadcasted_iota(jnp.int32, (rows, SEQ), 0) + pl.program_id(0) * rows
    c = lax.broadcasted_iota(jnp.int32, (rows, SEQ), 1)
    ang = ((r * c) & (NFFT - 1)).astype(F32) * (2.0 * math.pi / NFFT)
    fc_ref[...] = jnp.cos(ang).astype(BF16)
    fs_ref[...] = jnp.sin(ang).astype(BF16)


def _dft_matrices():
    rows = 256
    return pl.pallas_call(
        _dft_kernel,
        grid=(SEQ // rows,),
        out_specs=[pl.BlockSpec((rows, SEQ), lambda i: (i, 0))] * 2,
        out_shape=[jax.ShapeDtypeStruct((SEQ, SEQ), BF16)] * 2,
        compiler_params=_cparams(("arbitrary",)),
        name="dft_matrices",
    )()


def _proj_kernel(x_ref, xp_ref, xn_ref, nw_ref, why_ref, ws5_ref, wz_ref, wxbc_ref, wdt_ref, wdtt_ref, wg_ref,
                 hcw_ref, hcb_ref, scw_ref, scb_ref,
                 hy_ref, s5_ref, z_ref, xbc_ref, dt_ref, dtt_ref, g_ref, stage_a, stage_b):
    tm = x_ref.shape[0]
    tiles_per_seq = SEQ // tm
    pos = pl.program_id(0) % tiles_per_seq
    xp = jnp.where(pos == 0, 0.0, xp_ref[...])
    xn = jnp.where(pos == tiles_per_seq - 1, 0.0, xn_ref[...])
    hext = _rms(jnp.concatenate([xp, x_ref[...], xn], axis=0), nw_ref[...]).astype(BF16)
    ext = tm + 2 * FFN_HALO
    h = hext[FFN_HALO:FFN_HALO + tm]

    def conv(w_ref, cw_ref, cb_ref, cols, stage):
        stage[...] = _dot(hext, w_ref[:, cols])
        taps = cw_ref.shape[0]
        half = taps // 2
        acc = None
        for k in range(taps):
            term = stage[pl.ds(FFN_HALO + k - half, tm), :].reshape(tm // 8, 8, ctile) * cw_ref[k, :, cols]
            acc = term if acc is None else acc + term
        return (acc + cb_ref[:, cols]).reshape(tm, ctile)

    ctile = stage_a.shape[1]
    nconv = 3 * HY_WIDTH // ctile
    for j in range(2 * nconv):
        cols = slice(ctile * (j % nconv), ctile * (j % nconv + 1))
        stage = stage_a if j % 2 == 0 else stage_b
        if j < nconv:
            hy_ref[:, cols] = conv(why_ref, hcw_ref, hcb_ref, cols, stage).astype(BF16)
        else:
            xbc = conv(wxbc_ref, scw_ref, scb_ref, cols, stage)
            xbc_ref[:, cols] = (xbc * jax.nn.sigmoid(xbc)).astype(BF16)
        gcols = slice(ctile * j, ctile * (j + 1))
        g_ref[:, gcols] = jax.nn.sigmoid(_dot(h, wg_ref[:, gcols])).astype(BF16)
    u = _dot(h, ws5_ref[...])
    for j in range(S5_SLABS):
        s5_ref[j] = u[:, 128 * j:128 * (j + 1)]
    z = _dot(h, wz_ref[...])
    z_ref[...] = (z * jax.nn.sigmoid(z)).astype(BF16)
    dt_ref[...] = _dot(h, wdt_ref[...])
    dtt_ref[...] = _dot_nt(wdtt_ref[...], h)


def _proj(x2, nw, w):
    t = x2.shape[0]
    tm = PROJ_TM
    hb = tm // FFN_HALO
    last = t // FFN_HALO - 1
    row = lambda n: pl.BlockSpec((tm, n), lambda i: (i, 0))
    names = ("hy", "s5", "z", "xbc", "dt", "dtt", "gate", "hcw", "hcb", "scw", "scb")
    return pl.pallas_call(
        _proj_kernel,
        grid=(t // tm,),
        in_specs=[row(D_MODEL),
                  pl.BlockSpec((FFN_HALO, D_MODEL), lambda i: (jnp.maximum(i * hb - 1, 0), 0)),
                  pl.BlockSpec((FFN_HALO, D_MODEL), lambda i: (jnp.minimum((i + 1) * hb, last), 0)),
                  _const_spec((1, D_MODEL))] + [_const_spec(w[n].shape) for n in names],
        out_specs=[row(3 * HY_WIDTH),
                   pl.BlockSpec((S5_SLABS, tm, 128), lambda i: (0, i, 0)),
                   row(SSD_WIDTH), row(SSD_XBC), row(2 * SSD_HEADS),
                   pl.BlockSpec((2 * SSD_HEADS, tm), lambda i: (0, i)),
                   row(3 * D_MODEL)],
        out_shape=[jax.ShapeDtypeStruct((t, 3 * HY_WIDTH), BF16),
                   jax.ShapeDtypeStruct((S5_SLABS, t, 128), F32),
                   jax.ShapeDtypeStruct((t, SSD_WIDTH), BF16),
                   jax.ShapeDtypeStruct((t, SSD_XBC), BF16),
                   jax.ShapeDtypeStruct((t, 2 * SSD_HEADS), F32),
                   jax.ShapeDtypeStruct((2 * SSD_HEADS, t), F32),
                   jax.ShapeDtypeStruct((t, 3 * D_MODEL), BF16)],
        scratch_shapes=[pltpu.VMEM((tm + 2 * FFN_HALO, 256), F32)] * 2,
        compiler_params=_cparams(("arbitrary",)),
        name="in_proj",
    )(x2, x2, x2, nw, *[w[n] for n in names])


def _hyena_kernel(x0_ref, x1_ref, v_ref, fc_ref, fs_ref, kre_ref, kim_ref, knyq_ref, bias_ref, o_ref):
    row = lax.broadcasted_iota(jnp.int32, (SEQ, HY_TILE), 0)
    s = v_ref[...].astype(F32) * x1_ref[...].astype(F32)
    sb = s.astype(BF16)
    y = None
    for j in range(SEQ // HY_FTILE):
        fr = slice(HY_FTILE * j, HY_FTILE * (j + 1))
        sre = _dot(fc_ref[fr, :], sb)
        sim = _dot(fs_ref[fr, :], sb)
        kre = kre_ref[fr, :]
        kim = kim_ref[fr, :]
        yre = (sre * kre + sim * kim).astype(BF16)
        yim = (sim * kre - sre * kim).astype(BF16)
        part = _dot(fc_ref[:, fr], yre) + _dot(fs_ref[:, fr], yim)
        y = part if y is None else y + part
    alt = jnp.where((row & 1) == 0, 1.0, -1.0)
    snyq = jnp.sum(s * alt, axis=0, keepdims=True)
    y = y + alt * (snyq * knyq_ref[...])
    o_ref[...] = (x0_ref[...].astype(F32) * (y + bias_ref[...] * s)).astype(o_ref.dtype)


def _hyena(hy_u, hw, fc, fs):
    t = hy_u.shape[0]
    nb = t // SEQ
    nj = HY_WIDTH // HY_TILE
    part = lambda p: pl.BlockSpec((SEQ, HY_TILE), lambda j, b, p=p: (b, p * nj + j))
    ctile = lambda r: pl.BlockSpec((r, HY_TILE), lambda j, b: (0, j))
    return pl.pallas_call(
        _hyena_kernel,
        grid=(nj, nb),
        in_specs=[part(0), part(1), part(2),
                  _const_spec((SEQ, SEQ)), _const_spec((SEQ, SEQ)),
                  ctile(SEQ), ctile(SEQ), ctile(1), ctile(1)],
        out_specs=pl.BlockSpec((SEQ, HY_TILE), lambda j, b: (b, j)),
        out_shape=jax.ShapeDtypeStruct((t, HY_WIDTH), BF16),
        compiler_params=_cparams(("arbitrary", "arbitrary")),
        name="hyena",
    )(hy_u, hy_u, hy_u, fc, fs, hw["kre"], hw["kim"], hw["knyq"], hw["bias"])


def _s5_kernel(u_ref, tb_ref, cg_ref, sc_ref, dcol_ref, wgt_ref, bcol_ref, o_ref, ut_scr, yt_scr):
    nk = S5_NCHUNK
    for t in range(S5_CHUNK):
        for j in range(S5_SLABS):
            blk = u_ref[j, pl.ds(t, nk, stride=S5_CHUNK), :]
            ut_scr[t, 128 * j:128 * (j + 1), :] = blk.T

    half = S5_STATE
    gb = S5_GB
    width = gb * half
    row = lax.broadcasted_iota(jnp.int32, (nk, width), 0)

    def from_prev(x, s):
        if s % 8 == 0:
            return jnp.concatenate([jnp.zeros((s, width), F32), x[:nk - s]], axis=0)
        return jnp.where(row >= s, pltpu.roll(x, s, 0), 0.0)

    def from_next(x, s):
        if s % 8 == 0:
            return jnp.concatenate([x[s:], jnp.zeros((s, width), F32)], axis=0)
        return jnp.where(row < nk - s, pltpu.roll(x, nk - s, 0), 0.0)

    def group_block(p, carry):
        rs = []
        for i in range(gb):
            r0 = pl.multiple_of((gb * p + i) * S5_GROUP, S5_GROUP)
            ucol = ut_scr[:, pl.ds(r0, S5_GROUP), :].reshape(S5_CHUNK * S5_GROUP, nk).astype(BF16)
            rs.append(_dot(tb_ref[gb * p + i], ucol))

        def states(lo):
            return jnp.concatenate([r[lo:lo + half] for r in rs], axis=0).T

        fr, br, fi, bi = states(256), states(256 + half), states(256 + 2 * half), states(256 + 3 * half)
        sc = sc_ref[p]
        for lvl in range(7):
            s = 1 << lvl
            arf, aif, arb, aib = (sc[4 * lvl + i:4 * lvl + i + 1] for i in range(4))
            pfr, pfi = from_prev(fr, s), from_prev(fi, s)
            pbr, pbi = from_next(br, s), from_next(bi, s)
            fr, fi = fr + arf * pfr - aif * pfi, fi + arf * pfi + aif * pfr
            br, bi = br + arb * pbr - aib * pbi, bi + arb * pbi + aib * pbr
        carried = [from_prev(fr, 1).T, from_next(br, 1).T, from_prev(fi, 1).T, from_next(bi, 1).T]
        for i in range(gb):
            r0 = pl.multiple_of((gb * p + i) * S5_GROUP, S5_GROUP)
            hs = jnp.concatenate([c[half * i:half * (i + 1)] for c in carried], axis=0).astype(BF16)
            z = rs[i][0:256] + _dot(cg_ref[gb * p + i], hs)
            yt_scr[:, pl.ds(r0, S5_GROUP), :] = z.reshape(S5_CHUNK, S5_GROUP, nk)
        return carry

    lax.fori_loop(0, S5_GROUPS // gb, group_block, 0)

    for t in range(S5_CHUNK):
        gl = jax.nn.gelu(yt_scr[t] + dcol_ref[...] * ut_scr[t])
        z2 = _dot(wgt_ref[...], gl.astype(BF16)) + bcol_ref[...]
        yb = gl * jax.nn.sigmoid(z2)
        for j in range(S5_SLABS):
            o_ref[j, pl.ds(t, nk, stride=S5_CHUNK), :] = yb[128 * j:128 * (j + 1), :].T


def _s5(s5_u, sw):
    t = s5_u.shape[1]
    nb = t // SEQ
    slab = pl.BlockSpec((S5_SLABS, SEQ, 128), lambda b: (0, b, 0))
    return pl.pallas_call(
        _s5_kernel,
        grid=(nb,),
        in_specs=[slab, _const_spec(sw["tb"].shape), _const_spec(sw["cg"].shape), _const_spec(sw["sc"].shape),
                  _const_spec(sw["dcol"].shape), _const_spec(sw["wgt"].shape), _const_spec(sw["bcol"].shape)],
        out_specs=slab,
        out_shape=jax.ShapeDtypeStruct((S5_SLABS, t, 128), F32),
        scratch_shapes=[pltpu.VMEM((S5_CHUNK, S5_WIDTH, S5_NCHUNK), F32),
                        pltpu.VMEM((S5_CHUNK, S5_WIDTH, S5_NCHUNK), F32)],
        compiler_params=_cparams(("arbitrary",)),
        name="s5",
    )(s5_u, sw["tb"], sw["cg"], sw["sc"], sw["dcol"], sw["wgt"], sw["bcol"])


def _ssd_kernel(xs_scr, bc_scr, dt_ref, dtt_ref, a_ref, at_ref, db_ref, dbt_ref, dexp_ref,
                ef_ref, eb_ref, tril_ref, triu_ref, o_ref,
                y_scr, carry_scr, dt_scr, a_scr, dtt_scr, att_scr, cs_scr):
    q = SSD_CHUNK
    nh = SSD_HEADS

    dtv = jax.nn.softplus(dt_ref[...] + db_ref[...])
    dt_scr[...] = dtv
    a_scr[...] = dtv * a_ref[...]
    dtvt = jax.nn.softplus(dtt_ref[...] + dbt_ref[...])
    avt = dtvt * at_ref[...]
    for c in range(SSD_NCHUNK):
        dtt_scr[c] = dtvt[:, q * c:q * (c + 1)]
        att_scr[c] = avt[:, q * c:q * (c + 1)]

    ri = lax.broadcasted_iota(jnp.int32, (q, q), 0)
    ci = lax.broadcasted_iota(jnp.int32, (q, q), 1)
    causal = ri >= ci
    anti = ci >= ri
    col_is_f = lax.broadcasted_iota(jnp.int32, (q, 2 * nh), 1) < nh
    row_is_f = lax.broadcasted_iota(jnp.int32, (2 * nh, q), 0) < nh
    lane_lo = lax.broadcasted_iota(jnp.int32, (q, 128), 1) < SSD_HEADDIM
    tril = tril_ref[...]
    triu = triu_ref[...]

    def split2(a):
        a1 = a.astype(BF16)
        return a1, (a - a1.astype(F32)).astype(BF16)

    def col_cumsum(a_c):
        parts = split2(a_c)
        lo = sum(_dot(tril, p) for p in parts)
        hi = sum(_dot(triu, p) for p in parts)
        return jnp.where(col_is_f, lo, hi)

    def row_cumsum(at_c):
        parts = split2(at_c)
        lo = sum(_dot(p, triu) for p in parts)
        hi = sum(_dot(p, tril) for p in parts)
        return jnp.where(row_is_f, lo, hi)

    def expand(v, e_ref):
        return sum(_dot(p, e_ref[...]) for p in split2(v))

    def inter_chunk(c, cs, e_ref):
        r0 = pl.multiple_of(c * q, q)
        total = jnp.where(col_is_f[0:1], cs[q - 1:q], cs[0:1])
        w_state = _dot((dt_scr[pl.ds(r0, q), :] * jnp.exp(total - cs)).astype(BF16), e_ref[...])
        w_in = _dot(jnp.exp(cs).astype(BF16), e_ref[...])
        decay = expand(jnp.broadcast_to(jnp.exp(total), (8, 2 * nh)), e_ref)[0:1]
        xdd = (xs_scr[pl.ds(r0, q), :] * w_state).astype(BF16)
        outs = []
        for g in range(SSD_GROUPS):
            bg = bc_scr[pl.ds(r0, q), SSD_STATE * g:SSD_STATE * (g + 1)]
            cg = bc_scr[pl.ds(r0, q), SSD_BC + SSD_STATE * g:SSD_BC + SSD_STATE * (g + 1)]
            lanes = slice(512 * g, 512 * (g + 1))
            prev = carry_scr[g]
            outs.append(_dot(cg, prev.astype(BF16)) * w_in[:, lanes])
            carry_scr[g] = prev * decay[:, lanes] + _dot_tn(bg, xdd[:, lanes])
        return r0, jnp.concatenate(outs, axis=1)

    def forward_chunk(c, carry):
        r0 = pl.multiple_of(c * q, q)
        cs = col_cumsum(a_scr[pl.ds(r0, q), :])
        cs_scr[pl.ds(r0, q), :] = cs
        rs = row_cumsum(att_scr[c])
        dtt_c = dtt_scr[c]
        xs_b = xs_scr[pl.ds(r0, q), :]
        pairs = []
        for g in range(SSD_GROUPS):
            bg = bc_scr[pl.ds(r0, q), SSD_STATE * g:SSD_STATE * (g + 1)]
            cg = bc_scr[pl.ds(r0, q), SSD_BC + SSD_STATE * g:SSD_BC + SSD_STATE * (g + 1)]
            scores = _dot_nt(cg, bg)
            for j in range(HEADS_PER_GROUP // 2):
                ms = []
                for h in (HEADS_PER_GROUP * g + 2 * j, HEADS_PER_GROUP * g + 2 * j + 1):
                    hb = nh + h
                    lf = jnp.where(causal, jnp.exp(cs[:, h:h + 1] - rs[h:h + 1, :]), 0.0) * dtt_c[h:h + 1, :]
                    lb = jnp.where(anti, jnp.exp(cs[:, hb:hb + 1] - rs[hb:hb + 1, :]), 0.0) * dtt_c[hb:hb + 1, :]
                    ms.append((scores * (lf + lb)).astype(BF16))
                blk = HEADS_PER_GROUP // 2 * g + j
                xp = xs_b[:, 128 * blk:128 * (blk + 1)]
                rhs = jnp.concatenate([jnp.where(lane_lo, xp, 0), jnp.where(lane_lo, 0, xp)], axis=0)
                pairs.append(_dot(jnp.concatenate(ms, axis=1), rhs))
        _, y_off = inter_chunk(c, cs, ef_ref)
        y_scr[pl.ds(r0, q), :] = jnp.concatenate(pairs, axis=1) + y_off
        return carry

    def backward_chunk(i, carry):
        c = SSD_NCHUNK - 1 - i
        r0, y_off = inter_chunk(c, cs_scr[pl.ds(pl.multiple_of(c * q, q), q), :], eb_ref)
        y_scr[pl.ds(r0, q), :] += y_off
        return carry

    carry_scr[...] = jnp.zeros_like(carry_scr)
    lax.fori_loop(0, SSD_NCHUNK, forward_chunk, 0, unroll=2)
    carry_scr[...] = jnp.zeros_like(carry_scr)
    lax.fori_loop(0, SSD_NCHUNK, backward_chunk, 0, unroll=2)

    rows = 256
    for i in range(SEQ // rows):
        sl = slice(rows * i, rows * (i + 1))
        o_ref[sl, :] = (y_scr[sl, :] + dexp_ref[...] * xs_scr[sl, :]).astype(o_ref.dtype)


def _ssd(xbc, dt, dtt, sw):
    t = xbc.shape[0]
    nb = t // SEQ
    row = lambda n: pl.BlockSpec((SEQ, n), lambda b: (b, 0))
    names = ("a", "at", "db", "dbt", "dexp", "ef", "eb", "tril", "triu")
    return pl.pallas_call(
        _ssd_kernel,
        grid=(nb,),
        in_specs=[row(SSD_WIDTH),
                  pl.BlockSpec((SEQ, 2 * SSD_BC), lambda b: (b, SSD_WIDTH // (2 * SSD_BC))),
                  row(2 * SSD_HEADS),
                  pl.BlockSpec((2 * SSD_HEADS, SEQ), lambda b: (0, b))]
                 + [_const_spec(sw[n].shape) for n in names],
        out_specs=row(SSD_WIDTH),
        out_shape=jax.ShapeDtypeStruct((t, SSD_WIDTH), BF16),
        scratch_shapes=[pltpu.VMEM((SEQ, SSD_WIDTH), F32),
                        pltpu.VMEM((SSD_GROUPS, SSD_STATE, SSD_WIDTH // SSD_GROUPS), F32),
                        pltpu.VMEM((SEQ, 2 * SSD_HEADS), F32),
                        pltpu.VMEM((SEQ, 2 * SSD_HEADS), F32),
                        pltpu.VMEM((SSD_NCHUNK, 2 * SSD_HEADS, SSD_CHUNK), F32),
                        pltpu.VMEM((SSD_NCHUNK, 2 * SSD_HEADS, SSD_CHUNK), F32),
                        pltpu.VMEM((SEQ, 2 * SSD_HEADS), F32)],
        compiler_params=_cparams(("arbitrary",)),
        name="ssd",
    )(xbc, xbc, dt, dtt, *[sw[n] for n in names])


def _merge_kernel(x_ref, ya_ref, yb_ref, yc_ref, z_ref, g_ref, pa_ref, pb_ref, pc_ref, wo_ref, nc_ref, nw_ref,
                  xo_ref, h_ref):
    gate = g_ref[...].astype(F32)
    yb = jnp.concatenate([yb_ref[j] for j in range(S5_SLABS)], axis=1).astype(BF16)
    yc = _rms(yc_ref[...].astype(F32) * z_ref[...].astype(F32), nc_ref[...]).astype(BF16)
    merged = (gate[:, 0:D_MODEL] * _dot(ya_ref[...], pa_ref[...])
              + gate[:, D_MODEL:2 * D_MODEL] * _dot(yb, pb_ref[...])
              + gate[:, 2 * D_MODEL:] * _dot(yc, pc_ref[...]))
    xn = x_ref[...] + _dot(merged.astype(BF16), wo_ref[...])
    xo_ref[...] = xn
    h_ref[...] = _rms(xn, nw_ref[...]).astype(BF16)


def _merge(x2, ya, yb, yc, z, gates, mw):
    t = x2.shape[0]
    tm = PROJ_TM
    row = lambda n: pl.BlockSpec((tm, n), lambda i: (i, 0))
    return pl.pallas_call(
        _merge_kernel,
        grid=(t // tm,),
        in_specs=[row(D_MODEL), row(HY_WIDTH), pl.BlockSpec((S5_SLABS, tm, 128), lambda i: (0, i, 0)),
                  row(SSD_WIDTH), row(SSD_WIDTH), row(3 * D_MODEL),
                  _const_spec(mw["pa"].shape), _const_spec(mw["pb"].shape), _const_spec(mw["pc"].shape),
                  _const_spec(mw["wo"].shape), _const_spec((1, SSD_WIDTH)), _const_spec((1, D_MODEL))],
        out_specs=[row(D_MODEL), row(D_MODEL)],
        out_shape=[jax.ShapeDtypeStruct((t, D_MODEL), F32), jax.ShapeDtypeStruct((t, D_MODEL), BF16)],
        compiler_params=_cparams(("arbitrary",)),
        name="merge",
    )(x2, ya, yb, yc, z, gates, mw["pa"], mw["pb"], mw["pc"], mw["wo"], mw["nc"], mw["nw"])


def _ffn_kernel(h_ref, hp_ref, hn_ref, x_ref, wu_ref, cw_ref, cb_ref, wd_ref, nf_ref, o_ref, stage_a, stage_b,
                *, final_norm):
    tm = h_ref.shape[0]
    tiles_per_seq = SEQ // tm
    pos = pl.program_id(0) % tiles_per_seq
    prev = jnp.where(pos == 0, jnp.zeros_like(hp_ref[...]), hp_ref[...])
    nxt = jnp.where(pos == tiles_per_seq - 1, jnp.zeros_like(hn_ref[...]), hn_ref[...])
    hext = jnp.concatenate([prev, h_ref[...], nxt], axis=0)

    def conv(cols, stage):
        stage[...] = _dot(hext, wu_ref[:, cols])
        acc = None
        for k in range(3):
            term = stage[pl.ds(FFN_HALO + k - 1, tm), :].reshape(tm // 8, 8, FFN_TILE) * cw_ref[k, :, cols]
            acc = term if acc is None else acc + term
        return (acc + cb_ref[:, cols]).reshape(tm, FFN_TILE)

    acts = []
    for j in range(FFN_NTILE):
        gate = conv(slice(FFN_TILE * j, FFN_TILE * (j + 1)), stage_a)
        val = conv(slice(FFN_HIDDEN + FFN_TILE * j, FFN_HIDDEN + FFN_TILE * (j + 1)), stage_b)
        acts.append((gate * jax.nn.sigmoid(gate) * val).astype(BF16))
    y = x_ref[...] + _dot(jnp.concatenate(acts, axis=1), wd_ref[...])
    o_ref[...] = _rms(y, nf_ref[...]) if final_norm else y


def _ffn(h2, x2, fw, nf, final_norm):
    t = x2.shape[0]
    tm = PROJ_TM
    hb = tm // FFN_HALO
    last = t // FFN_HALO - 1
    row = lambda n: pl.BlockSpec((tm, n), lambda i: (i, 0))
    return pl.pallas_call(
        functools.partial(_ffn_kernel, final_norm=final_norm),
        grid=(t // tm,),
        in_specs=[row(D_MODEL),
                  pl.BlockSpec((FFN_HALO, D_MODEL), lambda i: (jnp.maximum(i * hb - 1, 0), 0)),
                  pl.BlockSpec((FFN_HALO, D_MODEL), lambda i: (jnp.minimum((i + 1) * hb, last), 0)),
                  row(D_MODEL),
                  _const_spec(fw["up"].shape), _const_spec(fw["cw"].shape), _const_spec(fw["cb"].shape),
                  _const_spec(fw["down"].shape), _const_spec((1, D_MODEL))],
        out_specs=row(D_MODEL),
        out_shape=jax.ShapeDtypeStruct((t, D_MODEL), F32),
        scratch_shapes=[pltpu.VMEM((tm + 2 * FFN_HALO, FFN_TILE), F32)] * 2,
        compiler_params=_cparams(("arbitrary",)),
        name="conv_ffn",
    )(h2, h2, h2, x2, fw["up"], fw["cw"], fw["cb"], fw["down"], nf)


_HI = lax.Precision.HIGHEST


def _rep8(v):
    return jnp.broadcast_to(v[..., None, :], v.shape[:-1] + (8, v.shape[-1]))


def _prep_proj(w_in, hy_short_w, hy_short_b, ssd_conv_w, ssd_conv_b):
    o = [0, 3 * HY_WIDTH, 3 * HY_WIDTH + S5_WIDTH]
    o.append(o[-1] + SSD_WIDTH)
    o.append(o[-1] + SSD_XBC)
    o.append(o[-1] + 2 * SSD_HEADS)
    wb = w_in.astype(BF16)
    return {"hy": wb[:, o[0]:o[1]], "s5": wb[:, o[1]:o[2]], "z": wb[:, o[2]:o[3]], "xbc": wb[:, o[3]:o[4]],
            "dt": wb[:, o[4]:o[5]], "dtt": wb[:, o[4]:o[5]].T, "gate": wb[:, o[5]:],
            "hcw": _rep8(hy_short_w), "hcb": _rep8(hy_short_b), "scw": _rep8(ssd_conv_w), "scb": _rep8(ssd_conv_b)}


def _prep_hyena(w1, b1, freq, w2, b2, w3, decay, bias):
    pos = jnp.arange(SEQ, dtype=F32)
    tt = (pos / max(SEQ - 1, 1))[:, None]
    bands = jnp.linspace(1e-4, HY_BANDS - 1, HY_BANDS, dtype=F32)
    ang = (2.0 * math.pi / SEQ) * pos[:, None] * bands[None, :]
    zz = jnp.concatenate([tt, jnp.cos(ang), -jnp.sin(ang)], axis=-1)
    h = jnp.sin(freq[0] * (jnp.dot(zz, w1, precision=_HI) + b1))
    h = jnp.sin(freq[1] * (jnp.dot(h, w2, precision=_HI) + b2))
    h = jnp.dot(h, w3, precision=_HI) * jnp.exp(-tt * jnp.abs(decay))
    h = h.reshape(SEQ, 2, HY_WIDTH)
    k = jnp.concatenate([h[:, 0], jnp.zeros((1, HY_WIDTH), F32), h[:0:-1, 1]], axis=0)
    k = k / jnp.sum(jnp.abs(k), axis=0, keepdims=True)
    kf = jnp.fft.rfft(k, axis=0)
    wgt = jnp.where(jnp.arange(SEQ) == 0, 1.0, 2.0)[:, None] / NFFT
    return {"kre": jnp.real(kf[:SEQ]) * wgt, "kim": jnp.imag(kf[:SEQ]) * wgt,
            "knyq": jnp.real(kf[SEQ:]) / NFFT, "bias": bias.reshape(1, HY_WIDTH)}


def _prep_s5(a_re, a_im, log_dt, b_re, b_im, c_re, c_im, d, w_glu, b_glu):
    q, gs, ns, ng = S5_CHUNK, S5_GROUP, S5_STATE, S5_GROUPS
    dt = jnp.exp(log_dt)[:, :, None]
    mag = jnp.exp(a_re * dt)
    ab_r, ab_i = mag * jnp.cos(a_im * dt), mag * jnp.sin(a_im * dt)
    den = a_re * a_re + a_im * a_im
    f_r = ((ab_r - 1.0) * a_re + ab_i * a_im) / den
    f_i = (ab_i * a_re - (ab_r - 1.0) * a_im) / den
    fb_r = f_r[..., None] * b_re - f_i[..., None] * b_im
    fb_i = f_r[..., None] * b_im + f_i[..., None] * b_re

    def power(j):
        m = jnp.exp(j * a_re * dt)
        return m * jnp.cos(j * a_im * dt), m * jnp.sin(j * a_im * dt)

    p_r, p_i = zip(*[power(float(j)) for j in range(q + 1)])
    p_r, p_i = jnp.stack(p_r, 2), jnp.stack(p_i, 2)

    cp_r = c_re[:, :, None] * p_r[:, :, :q, None] - c_im[:, :, None] * p_i[:, :, :q, None]
    cp_i = c_re[:, :, None] * p_i[:, :, :q, None] + c_im[:, :, None] * p_r[:, :, :q, None]
    kd = (jnp.einsum("zgdcn,zgne->zgdce", cp_r, fb_r, precision=_HI)
          - jnp.einsum("zgdcn,zgne->zgdce", cp_i, fb_i, precision=_HI))
    lag = jnp.arange(q)[:, None] - jnp.arange(q)[None, :]
    dd = jnp.arange(q)[:, None, None]
    sel_f = (lag[None] == dd).astype(F32)
    sel_b = (-lag[None] == dd).astype(F32)
    toep = (jnp.einsum("dtu,gdce->gtcue", sel_f, kd[0], precision=_HI)
            + jnp.einsum("dtu,gdce->gtcue", sel_b, kd[1], precision=_HI)).reshape(ng, q * gs, q * gs)

    def inject(z, pw):
        pr, pi = p_r[z][:, pw], p_i[z][:, pw]
        re = pr[..., None] * fb_r[z][:, None] - pi[..., None] * fb_i[z][:, None]
        im = pr[..., None] * fb_i[z][:, None] + pi[..., None] * fb_r[z][:, None]
        to_rows = lambda v: v.transpose(0, 2, 1, 3).reshape(ng, ns, q * gs)
        return to_rows(re), to_rows(im)

    bf_r, bf_i = inject(0, jnp.arange(q - 1, -1, -1))
    bb_r, bb_i = inject(1, jnp.arange(q))
    tbm = jnp.concatenate([toep, bf_r, bb_r, bf_i, bb_i], axis=1).astype(BF16)

    def readout(z, pw):
        pr, pi = p_r[z][:, pw], p_i[z][:, pw]
        re = c_re[z][:, None] * pr[:, :, None] - c_im[z][:, None] * pi[:, :, None]
        im = c_re[z][:, None] * pi[:, :, None] + c_im[z][:, None] * pr[:, :, None]
        return re.reshape(ng, q * gs, ns), -im.reshape(ng, q * gs, ns)

    cf_r, cf_i = readout(0, jnp.arange(1, q + 1))
    cb_r, cb_i = readout(1, jnp.arange(q, 0, -1))
    cgm = jnp.concatenate([cf_r, cb_r, cf_i, cb_i], axis=2).astype(BF16)

    rows = []
    for lvl in range(7):
        pr, pi = power(float(q << lvl))
        rows += [v.reshape(ng // S5_GB, S5_GB * ns) for v in (pr[0], pi[0], pr[1], pi[1])]
    rows += [jnp.zeros_like(rows[0])] * 4
    sc = jnp.stack(rows, axis=1)
    ones = jnp.ones((1, S5_NCHUNK), F32)
    return {"tb": tbm, "cg": cgm, "sc": sc, "dcol": d[:, None] * ones, "wgt": w_glu.T.astype(BF16),
            "bcol": b_glu[:, None] * ones}


def _prep_ssd(a_log, dt_bias, d):
    nh = SSD_HEADS
    a = (-jnp.exp(a_log)).reshape(1, 2 * nh)
    db = dt_bias.reshape(1, 2 * nh)
    head_of_lane = jnp.arange(SSD_WIDTH) // SSD_HEADDIM
    onehot = (jnp.arange(nh)[:, None] == head_of_lane[None, :]).astype(BF16)
    zero = jnp.zeros_like(onehot)
    i = jnp.arange(SSD_CHUNK)
    return {"a": a, "at": a.T, "db": db, "dbt": db.T,
            "dexp": jnp.repeat(d, SSD_HEADDIM).reshape(1, SSD_WIDTH),
            "ef": jnp.concatenate([onehot, zero], 0), "eb": jnp.concatenate([zero, onehot], 0),
            "tril": (i[:, None] >= i[None, :]).astype(BF16), "triu": (i[:, None] <= i[None, :]).astype(BF16)}


def kernel(x, norm_mix, w_in, hy_short_w, hy_short_b, hy_w1, hy_b1, hy_freq, hy_w2, hy_b2, hy_w3, hy_decay, hy_bias, s5_a_re, s5_a_im, s5_log_dt, s5_b_re, s5_b_im, s5_c_re, s5_c_im, s5_d, s5_w_glu, s5_b_glu, ssd_conv_w, ssd_conv_b, ssd_a_log, ssd_dt_bias, ssd_d, ssd_norm, p_a, p_b, p_c, w_out, norm_ffn, ffn_up, ffn_conv_w, ffn_conv_b, ffn_down, norm_final):
    bsz, seq, dm = x.shape
    assert (seq, dm) == (SEQ, D_MODEL)
    x2 = x.reshape(bsz * seq, dm)
    fc, fs = _dft_matrices()
    nf = norm_final.reshape(1, D_MODEL)
    for l in range(DEPTH):
        pw = _prep_proj(w_in[l], hy_short_w[l], hy_short_b[l], ssd_conv_w[l], ssd_conv_b[l])
        hw = _prep_hyena(hy_w1[l], hy_b1[l], hy_freq[l], hy_w2[l], hy_b2[l], hy_w3[l], hy_decay[l], hy_bias[l])
        s5w = _prep_s5(s5_a_re[l], s5_a_im[l], s5_log_dt[l], s5_b_re[l], s5_b_im[l], s5_c_re[l], s5_c_im[l],
                       s5_d[l], s5_w_glu[l], s5_b_glu[l])
        sdw = _prep_ssd(ssd_a_log[l], ssd_dt_bias[l], ssd_d[l])
        mw = {"pa": p_a[l].astype(BF16), "pb": p_b[l].astype(BF16), "pc": p_c[l].astype(BF16),
              "wo": w_out[l].astype(BF16), "nc": ssd_norm[l].reshape(1, SSD_WIDTH),
              "nw": norm_ffn[l].reshape(1, D_MODEL)}
        fw = {"up": ffn_up[l].astype(BF16), "cw": _rep8(ffn_conv_w[l]), "cb": _rep8(ffn_conv_b[l]),
              "down": ffn_down[l].astype(BF16)}

        hy_u, s5_u, z, xbc, dt, dtt, gates = _proj(x2, norm_mix[l].reshape(1, D_MODEL), pw)
        ya = _hyena(hy_u, hw, fc, fs)
        yb = _s5(s5_u, s5w)
        yc = _ssd(xbc, dt, dtt, sdw)
        x_mid, h2 = _merge(x2, ya, yb, yc, z, gates, mw)
        x2 = _ffn(h2, x_mid, fw, nf, final_norm=(l == DEPTH - 1))
    return x2.reshape(bsz, seq, dm)
```

```python
import functools
import math

import jax
import jax.numpy as jnp
from jax import lax
from jax.experimental import pallas as pl
from jax.experimental.pallas import tpu as pltpu

F32 = jnp.float32
BF16 = jnp.bfloat16

D_MODEL = 1024
SEQ = 2048
DEPTH = 2
EPS = 1e-6

HY_WIDTH = 512
HY_BANDS = 16
NFFT = 2 * SEQ

S5_WIDTH = 512
S5_GROUP = 16
S5_GROUPS = S5_WIDTH // S5_GROUP
S5_STATE = 64
S5_CHUNK = 16
S5_NCHUNK = SEQ // S5_CHUNK
S5_SLABS = S5_WIDTH // 128
S5_GB = 8

SSD_WIDTH = 1024
SSD_HEADDIM = 64
SSD_HEADS = SSD_WIDTH // SSD_HEADDIM
SSD_GROUPS = 2
SSD_STATE = 128
SSD_BC = SSD_GROUPS * SSD_STATE
SSD_XBC = SSD_WIDTH + 2 * SSD_BC
SSD_CONV = 5
SSD_CHUNK = 128
SSD_NCHUNK = SEQ // SSD_CHUNK
HEADS_PER_GROUP = SSD_HEADS // SSD_GROUPS

FFN_HIDDEN = 2816
FFN_TILE = 256
FFN_NTILE = FFN_HIDDEN // FFN_TILE
ROW_HALO = 16

PROJ_TM = 512
HY_TILE = 256
HY_FTILE = 256
VMEM_LIMIT = 56 * 1024 * 1024


def _cparams(sem):
    return pltpu.CompilerParams(dimension_semantics=sem, vmem_limit_bytes=VMEM_LIMIT)


def _const_spec(shape):
    nd = len(shape)
    return pl.BlockSpec(shape, lambda *_: (0,) * nd, pipeline_mode=pl.Buffered(1))


def _rms(x, w):
    return x * lax.rsqrt(jnp.mean(x * x, axis=-1, keepdims=True) + EPS) * w


def _dot(a, b):
    return jnp.dot(a, b, preferred_element_type=F32)


def _dot_nt(a, b):
    return lax.dot_general(a, b, (((1,), (1,)), ((), ())), preferred_element_type=F32)


def _dot_tn(a, b):
    return lax.dot_general(a, b, (((0,), (0,)), ((), ())), preferred_element_type=F32)


def _split3(a):
    a1 = a.astype(BF16)
    r1 = a - a1.astype(F32)
    a2 = r1.astype(BF16)
    a3 = (r1 - a2.astype(F32)).astype(BF16)
    return a1, a2, a3


def _dft_kernel(fc_ref, fs_ref):
    rows = fc_ref.shape[0]
    r = lax.broadcasted_iota(jnp.int32, (rows, SEQ), 0) + pl.program_id(0) * rows
    c = lax.broadcasted_iota(jnp.int32, (rows, SEQ), 1)
    ang = ((r * c) & (NFFT - 1)).astype(F32) * (2.0 * math.pi / NFFT)
    fc_ref[...] = jnp.cos(ang).astype(BF16)
    fs_ref[...] = jnp.sin(ang).astype(BF16)


def _dft_matrices():
    rows = 256
    return pl.pallas_call(
        _dft_kernel,
        grid=(SEQ // rows,),
        out_specs=[pl.BlockSpec((rows, SEQ), lambda i: (i, 0))] * 2,
        out_shape=[jax.ShapeDtypeStruct((SEQ, SEQ), BF16)] * 2,
        compiler_params=_cparams(("arbitrary",)),
        name="dft_matrices",
    )()


def _proj_kernel(x_ref, xp_ref, xn_ref, nw_ref, why_ref, ws5_ref, wz_ref, wxbc_ref, wdt_ref, wdtt_ref, wg_ref,
                 hcw_ref, hcb_ref, scw_ref, scb_ref,
                 hy_ref, s5_ref, z_ref, xbc_ref, dt_ref, dtt_ref, g_ref, stage_a, stage_b):
    tm = x_ref.shape[0]
    tiles_per_seq = SEQ // tm
    pos = pl.program_id(0) % tiles_per_seq
    xp = jnp.where(pos == 0, 0.0, xp_ref[...])
    xn = jnp.where(pos == tiles_per_seq - 1, 0.0, xn_ref[...])
    hext = _rms(jnp.concatenate([xp, x_ref[...], xn], axis=0), nw_ref[...]).astype(BF16)
    h = hext[ROW_HALO:ROW_HALO + tm]

    def conv(w_ref, cw_ref, cb_ref, cols, stage):
        stage[...] = _dot(hext, w_ref[:, cols])
        taps = cw_ref.shape[0]
        half = taps // 2
        acc = None
        for k in range(taps):
            term = stage[pl.ds(ROW_HALO + k - half, tm), :].reshape(tm // 8, 8, ctile) * cw_ref[k, :, cols]
            acc = term if acc is None else acc + term
        return (acc + cb_ref[:, cols]).reshape(tm, ctile)

    ctile = stage_a.shape[1]
    nconv = 3 * HY_WIDTH // ctile
    for j in range(2 * nconv):
        cols = slice(ctile * (j % nconv), ctile * (j % nconv + 1))
        stage = stage_a if j % 2 == 0 else stage_b
        if j < nconv:
            hy_ref[:, cols] = conv(why_ref, hcw_ref, hcb_ref, cols, stage).astype(BF16)
        else:
            xbc = conv(wxbc_ref, scw_ref, scb_ref, cols, stage)
            xbc_ref[:, cols] = (xbc * jax.nn.sigmoid(xbc)).astype(BF16)
        gcols = slice(ctile * j, ctile * (j + 1))
        g_ref[:, gcols] = jax.nn.sigmoid(_dot(h, wg_ref[:, gcols])).astype(BF16)
    u = _dot(h, ws5_ref[...])
    for j in range(S5_SLABS):
        s5_ref[j] = u[:, 128 * j:128 * (j + 1)]
    z = _dot(h, wz_ref[...])
    z_ref[...] = (z * jax.nn.sigmoid(z)).astype(BF16)
    dt_ref[...] = _dot(h, wdt_ref[...])
    dtt_ref[...] = _dot_nt(wdtt_ref[...], h)


def _proj(x2, nw, w):
    t = x2.shape[0]
    tm = PROJ_TM
    hb = tm // ROW_HALO
    last = t // ROW_HALO - 1
    row = lambda n: pl.BlockSpec((tm, n), lambda i: (i, 0))
    names = ("hy", "s5", "z", "xbc", "dt", "dtt", "gate", "hcw", "hcb", "scw", "scb")
    return pl.pallas_call(
        _proj_kernel,
        grid=(t // tm,),
        in_specs=[row(D_MODEL),
                  pl.BlockSpec((ROW_HALO, D_MODEL), lambda i: (jnp.maximum(i * hb - 1, 0), 0)),
                  pl.BlockSpec((ROW_HALO, D_MODEL), lambda i: (jnp.minimum((i + 1) * hb, last), 0)),
                  _const_spec((1, D_MODEL))] + [_const_spec(w[n].shape) for n in names],
        out_specs=[row(3 * HY_WIDTH),
                   pl.BlockSpec((S5_SLABS, tm, 128), lambda i: (0, i, 0)),
                   row(SSD_WIDTH), row(SSD_XBC), row(2 * SSD_HEADS),
                   pl.BlockSpec((2 * SSD_HEADS, tm), lambda i: (0, i)),
                   row(3 * D_MODEL)],
        out_shape=[jax.ShapeDtypeStruct((t, 3 * HY_WIDTH), BF16),
                   jax.ShapeDtypeStruct((S5_SLABS, t, 128), F32),
                   jax.ShapeDtypeStruct((t, SSD_WIDTH), BF16),
                   jax.ShapeDtypeStruct((t, SSD_XBC), BF16),
                   jax.ShapeDtypeStruct((t, 2 * SSD_HEADS), F32),
                   jax.ShapeDtypeStruct((2 * SSD_HEADS, t), F32),
                   jax.ShapeDtypeStruct((t, 3 * D_MODEL), BF16)],
        scratch_shapes=[pltpu.VMEM((tm + 2 * ROW_HALO, 256), F32)] * 2,
        compiler_params=_cparams(("arbitrary",)),
        name="in_proj",
    )(x2, x2, x2, nw, *[w[n] for n in names])


def _hyena_kernel(x0_ref, x1_ref, v_ref, fc_ref, fs_ref, kre_ref, kim_ref, knyq_ref, bias_ref, o_ref):
    row = lax.broadcasted_iota(jnp.int32, (SEQ, HY_TILE), 0)
    s = v_ref[...].astype(F32) * x1_ref[...].astype(F32)
    sb = s.astype(BF16)
    y = None
    for j in range(SEQ // HY_FTILE):
        fr = slice(HY_FTILE * j, HY_FTILE * (j + 1))
        sre = _dot(fc_ref[fr, :], sb)
        sim = _dot(fs_ref[fr, :], sb)
        kre = kre_ref[fr, :]
        kim = kim_ref[fr, :]
        yre = (sre * kre + sim * kim).astype(BF16)
        yim = (sim * kre - sre * kim).astype(BF16)
        part = _dot(fc_ref[:, fr], yre) + _dot(fs_ref[:, fr], yim)
        y = part if y is None else y + part
    alt = jnp.where((row & 1) == 0, 1.0, -1.0)
    snyq = jnp.sum(s * alt, axis=0, keepdims=True)
    y = y + alt * (snyq * knyq_ref[...])
    o_ref[...] = (x0_ref[...].astype(F32) * (y + bias_ref[...] * s)).astype(o_ref.dtype)


def _hyena(hy_u, hw, fc, fs):
    t = hy_u.shape[0]
    nb = t // SEQ
    nj = HY_WIDTH // HY_TILE
    part = lambda p: pl.BlockSpec((SEQ, HY_TILE), lambda j, b, p=p: (b, p * nj + j))
    ctile = lambda r: pl.BlockSpec((r, HY_TILE), lambda j, b: (0, j))
    return pl.pallas_call(
        _hyena_kernel,
        grid=(nj, nb),
        in_specs=[part(0), part(1), part(2),
                  _const_spec((SEQ, SEQ)), _const_spec((SEQ, SEQ)),
                  ctile(SEQ), ctile(SEQ), ctile(1), ctile(1)],
        out_specs=pl.BlockSpec((SEQ, HY_TILE), lambda j, b: (b, j)),
        out_shape=jax.ShapeDtypeStruct((t, HY_WIDTH), BF16),
        compiler_params=_cparams(("arbitrary", "arbitrary")),
        name="hyena",
    )(hy_u, hy_u, hy_u, fc, fs, hw["kre"], hw["kim"], hw["knyq"], hw["bias"])


def _s5_kernel(u_ref, tb_ref, cg_ref, sc_ref, dcol_ref, wgt_ref, bcol_ref, o_ref, ut_scr, yt_scr):
    nk = S5_NCHUNK
    for t in range(S5_CHUNK):
        for j in range(S5_SLABS):
            blk = u_ref[j, pl.ds(t, nk, stride=S5_CHUNK), :]
            ut_scr[t, 128 * j:128 * (j + 1), :] = blk.T

    half = S5_STATE
    gb = S5_GB
    width = gb * half
    row = lax.broadcasted_iota(jnp.int32, (nk, width), 0)

    def from_prev(x, s):
        if s % 8 == 0:
            return jnp.concatenate([jnp.zeros((s, width), F32), x[:nk - s]], axis=0)
        return jnp.where(row >= s, pltpu.roll(x, s, 0), 0.0)

    def from_next(x, s):
        if s % 8 == 0:
            return jnp.concatenate([x[s:], jnp.zeros((s, width), F32)], axis=0)
        return jnp.where(row < nk - s, pltpu.roll(x, nk - s, 0), 0.0)

    def group_block(p, carry):
        rs = []
        for i in range(gb):
            r0 = pl.multiple_of((gb * p + i) * S5_GROUP, S5_GROUP)
            ucol = ut_scr[:, pl.ds(r0, S5_GROUP), :].reshape(S5_CHUNK * S5_GROUP, nk).astype(BF16)
            rs.append(_dot(tb_ref[gb * p + i], ucol))

        def states(lo):
            return jnp.concatenate([r[lo:lo + half] for r in rs], axis=0).T

        fr, br, fi, bi = states(256), states(256 + half), states(256 + 2 * half), states(256 + 3 * half)
        sc = sc_ref[p]
        for lvl in range(7):
            s = 1 << lvl
            arf, aif, arb, aib = (sc[4 * lvl + i:4 * lvl + i + 1] for i in range(4))
            pfr, pfi = from_prev(fr, s), from_prev(fi, s)
            pbr, pbi = from_next(br, s), from_next(bi, s)
            fr, fi = fr + arf * pfr - aif * pfi, fi + arf * pfi + aif * pfr
            br, bi = br + arb * pbr - aib * pbi, bi + arb * pbi + aib * pbr
        carried = [from_prev(fr, 1).T, from_next(br, 1).T, from_prev(fi, 1).T, from_next(bi, 1).T]
        for i in range(gb):
            r0 = pl.multiple_of((gb * p + i) * S5_GROUP, S5_GROUP)
            hs = jnp.concatenate([c[half * i:half * (i + 1)] for c in carried], axis=0).astype(BF16)
            z = rs[i][0:256] + _dot(cg_ref[gb * p + i], hs)
            yt_scr[:, pl.ds(r0, S5_GROUP), :] = z.reshape(S5_CHUNK, S5_GROUP, nk)
        return carry

    lax.fori_loop(0, S5_GROUPS // gb, group_block, 0)

    for t in range(S5_CHUNK):
        gl = jax.nn.gelu(yt_scr[t] + dcol_ref[...] * ut_scr[t])
        z2 = _dot(wgt_ref[...], gl.astype(BF16)) + bcol_ref[...]
        yb = gl * jax.nn.sigmoid(z2)
        for j in range(S5_SLABS):
            o_ref[j, pl.ds(t, nk, stride=S5_CHUNK), :] = yb[128 * j:128 * (j + 1), :].T


def _s5(s5_u, sw):
    t = s5_u.shape[1]
    nb = t // SEQ
    slab = pl.BlockSpec((S5_SLABS, SEQ, 128), lambda b: (0, b, 0))
    return pl.pallas_call(
        _s5_kernel,
        grid=(nb,),
        in_specs=[slab, _const_spec(sw["tb"].shape), _const_spec(sw["cg"].shape), _const_spec(sw["sc"].shape),
                  _const_spec(sw["dcol"].shape), _const_spec(sw["wgt"].shape), _const_spec(sw["bcol"].shape)],
        out_specs=slab,
        out_shape=jax.ShapeDtypeStruct((S5_SLABS, t, 128), F32),
        scratch_shapes=[pltpu.VMEM((S5_CHUNK, S5_WIDTH, S5_NCHUNK), F32),
                        pltpu.VMEM((S5_CHUNK, S5_WIDTH, S5_NCHUNK), F32)],
        compiler_params=_cparams(("arbitrary",)),
        name="s5",
    )(s5_u, sw["tb"], sw["cg"], sw["sc"], sw["dcol"], sw["wgt"], sw["bcol"])


def _ssd_kernel(xs_scr, bc_scr, dt_ref, dtt_ref, a_ref, at_ref, db_ref, dbt_ref, dexp_ref,
                ef_ref, eb_ref, tril_ref, triu_ref, o_ref,
                y_scr, carry_scr, dt_scr, a_scr, dtt_scr, att_scr, cs_scr):
    q = SSD_CHUNK
    nh = SSD_HEADS

    dtv = jax.nn.softplus(dt_ref[...] + db_ref[...])
    dt_scr[...] = dtv
    a_scr[...] = dtv * a_ref[...]
    dtvt = jax.nn.softplus(dtt_ref[...] + dbt_ref[...])
    avt = dtvt * at_ref[...]
    for c in range(SSD_NCHUNK):
        dtt_scr[c] = dtvt[:, q * c:q * (c + 1)]
        att_scr[c] = avt[:, q * c:q * (c + 1)]

    ri = lax.broadcasted_iota(jnp.int32, (q, q), 0)
    ci = lax.broadcasted_iota(jnp.int32, (q, q), 1)
    causal = ri >= ci
    anti = ci >= ri
    col_is_f = lax.broadcasted_iota(jnp.int32, (q, 2 * nh), 1) < nh
    row_is_f = lax.broadcasted_iota(jnp.int32, (2 * nh, q), 0) < nh
    lane_lo = lax.broadcasted_iota(jnp.int32, (q, 128), 1) < SSD_HEADDIM
    tril = tril_ref[...]
    triu = triu_ref[...]

    def split2(a):
        a1 = a.astype(BF16)
        return a1, (a - a1.astype(F32)).astype(BF16)

    def col_cumsum(a_c):
        parts = split2(a_c)
        lo = sum(_dot(tril, p) for p in parts)
        hi = sum(_dot(triu, p) for p in parts)
        return jnp.where(col_is_f, lo, hi)

    def row_cumsum(at_c):
        parts = split2(at_c)
        lo = sum(_dot(p, triu) for p in parts)
        hi = sum(_dot(p, tril) for p in parts)
        return jnp.where(row_is_f, lo, hi)

    def expand(v, e_ref):
        return sum(_dot(p, e_ref[...]) for p in split2(v))

    def inter_chunk(c, cs, e_ref):
        r0 = pl.multiple_of(c * q, q)
        total = jnp.where(col_is_f[0:1], cs[q - 1:q], cs[0:1])
        w_state = _dot((dt_scr[pl.ds(r0, q), :] * jnp.exp(total - cs)).astype(BF16), e_ref[...])
        w_in = _dot(jnp.exp(cs).astype(BF16), e_ref[...])
        decay = expand(jnp.broadcast_to(jnp.exp(total), (8, 2 * nh)), e_ref)[0:1]
        xdd = (xs_scr[pl.ds(r0, q), :] * w_state).astype(BF16)
        outs = []
        for g in range(SSD_GROUPS):
            bg = bc_scr[pl.ds(r0, q), SSD_STATE * g:SSD_STATE * (g + 1)]
            cg = bc_scr[pl.ds(r0, q), SSD_BC + SSD_STATE * g:SSD_BC + SSD_STATE * (g + 1)]
            lanes = slice(512 * g, 512 * (g + 1))
            prev = carry_scr[g]
            outs.append(_dot(cg, prev.astype(BF16)) * w_in[:, lanes])
            carry_scr[g] = prev * decay[:, lanes] + _dot_tn(bg, xdd[:, lanes])
        return r0, jnp.concatenate(outs, axis=1)

    def forward_chunk(c, carry):
        r0 = pl.multiple_of(c * q, q)
        cs = col_cumsum(a_scr[pl.ds(r0, q), :])
        cs_scr[pl.ds(r0, q), :] = cs
        rs = row_cumsum(att_scr[c])
        dtt_c = dtt_scr[c]
        xs_b = xs_scr[pl.ds(r0, q), :]
        pairs = []
        for g in range(SSD_GROUPS):
            bg = bc_scr[pl.ds(r0, q), SSD_STATE * g:SSD_STATE * (g + 1)]
            cg = bc_scr[pl.ds(r0, q), SSD_BC + SSD_STATE * g:SSD_BC + SSD_STATE * (g + 1)]
            scores = _dot_nt(cg, bg)
            for j in range(HEADS_PER_GROUP // 2):
                ms = []
                for h in (HEADS_PER_GROUP * g + 2 * j, HEADS_PER_GROUP * g + 2 * j + 1):
                    hb = nh + h
                    lf = jnp.where(causal, jnp.exp(cs[:, h:h + 1] - rs[h:h + 1, :]), 0.0) * dtt_c[h:h + 1, :]
                    lb = jnp.where(anti, jnp.exp(cs[:, hb:hb + 1] - rs[hb:hb + 1, :]), 0.0) * dtt_c[hb:hb + 1, :]
                    ms.append((scores * (lf + lb)).astype(BF16))
                blk = HEADS_PER_GROUP // 2 * g + j
                xp = xs_b[:, 128 * blk:128 * (blk + 1)]
                rhs = jnp.concatenate([jnp.where(lane_lo, xp, 0), jnp.where(lane_lo, 0, xp)], axis=0)
                pairs.append(_dot(jnp.concatenate(ms, axis=1), rhs))
        _, y_off = inter_chunk(c, cs, ef_ref)
        y_scr[pl.ds(r0, q), :] = jnp.concatenate(pairs, axis=1) + y_off
        return carry

    def backward_chunk(i, carry):
        c = SSD_NCHUNK - 1 - i
        r0, y_off = inter_chunk(c, cs_scr[pl.ds(pl.multiple_of(c * q, q), q), :], eb_ref)
        y_scr[pl.ds(r0, q), :] += y_off
        return carry

    carry_scr[...] = jnp.zeros_like(carry_scr)
    lax.fori_loop(0, SSD_NCHUNK, forward_chunk, 0, unroll=2)
    carry_scr[...] = jnp.zeros_like(carry_scr)
    lax.fori_loop(0, SSD_NCHUNK, backward_chunk, 0, unroll=2)

    rows = 256
    for i in range(SEQ // rows):
        sl = slice(rows * i, rows * (i + 1))
        o_ref[sl, :] = (y_scr[sl, :] + dexp_ref[...] * xs_scr[sl, :]).astype(o_ref.dtype)


def _ssd(xbc, dt, dtt, sw):
    t = xbc.shape[0]
    nb = t // SEQ
    row = lambda n: pl.BlockSpec((SEQ, n), lambda b: (b, 0))
    names = ("a", "at", "db", "dbt", "dexp", "ef", "eb", "tril", "triu")
    return pl.pallas_call(
        _ssd_kernel,
        grid=(nb,),
        in_specs=[row(SSD_WIDTH),
                  pl.BlockSpec((SEQ, 2 * SSD_BC), lambda b: (b, SSD_WIDTH // (2 * SSD_BC))),
                  row(2 * SSD_HEADS),
                  pl.BlockSpec((2 * SSD_HEADS, SEQ), lambda b: (0, b))]
                 + [_const_spec(sw[n].shape) for n in names],
        out_specs=row(SSD_WIDTH),
        out_shape=jax.ShapeDtypeStruct((t, SSD_WIDTH), BF16),
        scratch_shapes=[pltpu.VMEM((SEQ, SSD_WIDTH), F32),
                        pltpu.VMEM((SSD_GROUPS, SSD_STATE, SSD_WIDTH // SSD_GROUPS), F32),
                        pltpu.VMEM((SEQ, 2 * SSD_HEADS), F32),
                        pltpu.VMEM((SEQ, 2 * SSD_HEADS), F32),
                        pltpu.VMEM((SSD_NCHUNK, 2 * SSD_HEADS, SSD_CHUNK), F32),
                        pltpu.VMEM((SSD_NCHUNK, 2 * SSD_HEADS, SSD_CHUNK), F32),
                        pltpu.VMEM((SEQ, 2 * SSD_HEADS), F32)],
        compiler_params=_cparams(("arbitrary",)),
        name="ssd",
    )(xbc, xbc, dt, dtt, *[sw[n] for n in names])


def _merge_kernel(x_ref, ya_ref, yb_ref, yc_ref, z_ref, g_ref, pa_ref, pb_ref, pc_ref, wo_ref, nc_ref, nw_ref,
                  xo_ref, h_ref):
    gate = g_ref[...].astype(F32)
    yb = jnp.concatenate([yb_ref[j] for j in range(S5_SLABS)], axis=1).astype(BF16)
    yc = _rms(yc_ref[...].astype(F32) * z_ref[...].astype(F32), nc_ref[...]).astype(BF16)
    merged = (gate[:, 0:D_MODEL] * _dot(ya_ref[...], pa_ref[...])
              + gate[:, D_MODEL:2 * D_MODEL] * _dot(yb, pb_ref[...])
              + gate[:, 2 * D_MODEL:] * _dot(yc, pc_ref[...]))
    xn = x_ref[...] + _dot(merged.astype(BF16), wo_ref[...])
    xo_ref[...] = xn
    h_ref[...] = _rms(xn, nw_ref[...]).astype(BF16)


def _merge(x2, ya, yb, yc, z, gates, mw):
    t = x2.shape[0]
    tm = PROJ_TM
    row = lambda n: pl.BlockSpec((tm, n), lambda i: (i, 0))
    return pl.pallas_call(
        _merge_kernel,
        grid=(t // tm,),
        in_specs=[row(D_MODEL), row(HY_WIDTH), pl.BlockSpec((S5_SLABS, tm, 128), lambda i: (0, i, 0)),
                  row(SSD_WIDTH), row(SSD_WIDTH), row(3 * D_MODEL),
                  _const_spec(mw["pa"].shape), _const_spec(mw["pb"].shape), _const_spec(mw["pc"].shape),
                  _const_spec(mw["wo"].shape), _const_spec((1, SSD_WIDTH)), _const_spec((1, D_MODEL))],
        out_specs=[row(D_MODEL), row(D_MODEL)],
        out_shape=[jax.ShapeDtypeStruct((t, D_MODEL), F32), jax.ShapeDtypeStruct((t, D_MODEL), BF16)],
        compiler_params=_cparams(("arbitrary",)),
        name="merge",
    )(x2, ya, yb, yc, z, gates, mw["pa"], mw["pb"], mw["pc"], mw["wo"], mw["nc"], mw["nw"])


def _ffn_kernel(h_ref, hp_ref, hn_ref, x_ref, wu_ref, cw_ref, cb_ref, wd_ref, nf_ref, o_ref, stage_a, stage_b,
                *, final_norm):
    tm = h_ref.shape[0]
    tiles_per_seq = SEQ // tm
    pos = pl.program_id(0) % tiles_per_seq
    prev = jnp.where(pos == 0, jnp.zeros_like(hp_ref[...]), hp_ref[...])
    nxt = jnp.where(pos == tiles_per_seq - 1, jnp.zeros_like(hn_ref[...]), hn_ref[...])
    hext = jnp.concatenate([prev, h_ref[...], nxt], axis=0)

    def conv(cols, stage):
        stage[...] = _dot(hext, wu_ref[:, cols])
        acc = None
        for k in range(3):
            term = stage[pl.ds(ROW_HALO + k - 1, tm), :].reshape(tm // 8, 8, FFN_TILE) * cw_ref[k, :, cols]
            acc = term if acc is None else acc + term
        return (acc + cb_ref[:, cols]).reshape(tm, FFN_TILE)

    acts = []
    for j in range(FFN_NTILE):
        gate = conv(slice(FFN_TILE * j, FFN_TILE * (j + 1)), stage_a)
        val = conv(slice(FFN_HIDDEN + FFN_TILE * j, FFN_HIDDEN + FFN_TILE * (j + 1)), stage_b)
        acts.append((gate * jax.nn.sigmoid(gate) * val).astype(BF16))
    y = x_ref[...] + _dot(jnp.concatenate(acts, axis=1), wd_ref[...])
    o_ref[...] = _rms(y, nf_ref[...]) if final_norm else y


def _ffn(h2, x2, fw, nf, final_norm):
    t = x2.shape[0]
    tm = PROJ_TM
    hb = tm // ROW_HALO
    last = t // ROW_HALO - 1
    row = lambda n: pl.BlockSpec((tm, n), lambda i: (i, 0))
    return pl.pallas_call(
        functools.partial(_ffn_kernel, final_norm=final_norm),
        grid=(t // tm,),
        in_specs=[row(D_MODEL),
                  pl.BlockSpec((ROW_HALO, D_MODEL), lambda i: (jnp.maximum(i * hb - 1, 0), 0)),
                  pl.BlockSpec((ROW_HALO, D_MODEL), lambda i: (jnp.minimum((i + 1) * hb, last), 0)),
                  row(D_MODEL),
                  _const_spec(fw["up"].shape), _const_spec(fw["cw"].shape), _const_spec(fw["cb"].shape),
                  _const_spec(fw["down"].shape), _const_spec((1, D_MODEL))],
        out_specs=row(D_MODEL),
        out_shape=jax.ShapeDtypeStruct((t, D_MODEL), F32),
        scratch_shapes=[pltpu.VMEM((tm + 2 * ROW_HALO, FFN_TILE), F32)] * 2,
        compiler_params=_cparams(("arbitrary",)),
        name="conv_ffn",
    )(h2, h2, h2, x2, fw["up"], fw["cw"], fw["cb"], fw["down"], nf)


_HI = lax.Precision.HIGHEST


def _rep8(v):
    return jnp.broadcast_to(v[..., None, :], v.shape[:-1] + (8, v.shape[-1]))


def _prep_proj(w_in, hy_short_w, hy_short_b, ssd_conv_w, ssd_conv_b):
    o = [0, 3 * HY_WIDTH, 3 * HY_WIDTH + S5_WIDTH]
    o.append(o[-1] + SSD_WIDTH)
    o.append(o[-1] + SSD_XBC)
    o.append(o[-1] + 2 * SSD_HEADS)
    cut = lambda a, b: w_in[:, a:b].astype(BF16)
    return {"hy": cut(o[0], o[1]), "s5": cut(o[1], o[2]), "z": cut(o[2], o[3]), "xbc": cut(o[3], o[4]),
            "dt": cut(o[4], o[5]), "dtt": cut(o[4], o[5]).T, "gate": cut(o[5], w_in.shape[1]),
            "hcw": _rep8(hy_short_w), "hcb": _rep8(hy_short_b), "scw": _rep8(ssd_conv_w), "scb": _rep8(ssd_conv_b)}


def _hyena_taps(w1, b1, freq, w2, b2, w3, decay):
    pos = jnp.arange(SEQ, dtype=F32)
    tt = (pos / max(SEQ - 1, 1))[:, None]
    bands = jnp.linspace(1e-4, HY_BANDS - 1, HY_BANDS, dtype=F32)
    ang = (2.0 * math.pi / SEQ) * pos[:, None] * bands[None, :]
    zz = jnp.concatenate([tt, jnp.cos(ang), -jnp.sin(ang)], axis=-1)
    h = jnp.sin(freq[0] * (jnp.dot(zz, w1, precision=_HI) + b1))
    h = jnp.sin(freq[1] * (jnp.dot(h, w2, precision=_HI) + b2))
    return jnp.dot(h, w3, precision=_HI) * jnp.exp(-tt * jnp.abs(decay))


def _spectrum_kernel(hf_ref, hb_ref, fc_ref, fs_ref, kre_ref, kim_ref, knyq_ref):
    row = lax.broadcasted_iota(jnp.int32, (SEQ, HY_TILE), 0)
    hf = hf_ref[...]
    hb = jnp.where(row == 0, 0.0, hb_ref[...])
    inv = 1.0 / (NFFT * (jnp.sum(jnp.abs(hf), axis=0, keepdims=True) + jnp.sum(jnp.abs(hb), axis=0, keepdims=True)))
    even = hf + hb
    odd = hb - hf
    dft = lambda m_ref, v: sum(_dot(m_ref[...], p) for p in _split3(v))
    freq = lax.broadcasted_iota(jnp.int32, (HY_FTILE, HY_TILE), 0) + pl.program_id(1) * HY_FTILE
    scale = jnp.where(freq == 0, 1.0, 2.0) * inv
    kre_ref[...] = dft(fc_ref, even) * scale
    kim_ref[...] = dft(fs_ref, odd) * scale
    alt = jnp.where((row & 1) == 0, 1.0, -1.0)
    knyq_ref[...] = jnp.sum(even * alt, axis=0, keepdims=True) * inv


def _hyena_spectrum(taps, fc, fs):
    nj = HY_WIDTH // HY_TILE
    col = lambda r, off: pl.BlockSpec((r, HY_TILE), lambda j, f, off=off: (0, off + j))
    frow = pl.BlockSpec((HY_FTILE, SEQ), lambda j, f: (f, 0))
    fout = pl.BlockSpec((HY_FTILE, HY_TILE), lambda j, f: (f, j))
    return pl.pallas_call(
        _spectrum_kernel,
        grid=(nj, SEQ // HY_FTILE),
        in_specs=[col(SEQ, 0), col(SEQ, nj), frow, frow],
        out_specs=[fout, fout, col(1, 0)],
        out_shape=[jax.ShapeDtypeStruct((SEQ, HY_WIDTH), F32), jax.ShapeDtypeStruct((SEQ, HY_WIDTH), F32),
                   jax.ShapeDtypeStruct((1, HY_WIDTH), F32)],
        compiler_params=_cparams(("arbitrary", "arbitrary")),
        name="hyena_spectrum",
    )(taps, taps, fc, fs)


def _prep_s5(a_re, a_im, log_dt, b_re, b_im, c_re, c_im, d, w_glu, b_glu):
    q, gs, ns, ng = S5_CHUNK, S5_GROUP, S5_STATE, S5_GROUPS
    dt = jnp.exp(log_dt)[:, :, None]
    mag = jnp.exp(a_re * dt)
    ab_r, ab_i = mag * jnp.cos(a_im * dt), mag * jnp.sin(a_im * dt)
    den = a_re * a_re + a_im * a_im
    f_r = ((ab_r - 1.0) * a_re + ab_i * a_im) / den
    f_i = (ab_i * a_re - (ab_r - 1.0) * a_im) / den
    fb_r = f_r[..., None] * b_re - f_i[..., None] * b_im
    fb_i = f_r[..., None] * b_im + f_i[..., None] * b_re

    def power(j):
        m = jnp.exp(j * a_re * dt)
        return m * jnp.cos(j * a_im * dt), m * jnp.sin(j * a_im * dt)

    p_r, p_i = zip(*[power(float(j)) for j in range(q + 1)])
    p_r, p_i = jnp.stack(p_r, 2), jnp.stack(p_i, 2)

    cp_r = c_re[:, :, None] * p_r[:, :, :q, None] - c_im[:, :, None] * p_i[:, :, :q, None]
    cp_i = c_re[:, :, None] * p_i[:, :, :q, None] + c_im[:, :, None] * p_r[:, :, :q, None]
    kd = (jnp.einsum("zgdcn,zgne->zgdce", cp_r, fb_r, precision=_HI)
          - jnp.einsum("zgdcn,zgne->zgdce", cp_i, fb_i, precision=_HI))
    lag = jnp.arange(q)[:, None] - jnp.arange(q)[None, :]
    dd = jnp.arange(q)[:, None, None]
    sel_f = (lag[None] == dd).astype(F32)
    sel_b = (-lag[None] == dd).astype(F32)
    toep = (jnp.einsum("dtu,gdce->gtcue", sel_f, kd[0], precision=_HI)
            + jnp.einsum("dtu,gdce->gtcue", sel_b, kd[1], precision=_HI)).reshape(ng, q * gs, q * gs)

    def inject(z, pw):
        pr, pi = p_r[z][:, pw], p_i[z][:, pw]
        re = pr[..., None] * fb_r[z][:, None] - pi[..., None] * fb_i[z][:, None]
        im = pr[..., None] * fb_i[z][:, None] + pi[..., None] * fb_r[z][:, None]
        to_rows = lambda v: v.transpose(0, 2, 1, 3).reshape(ng, ns, q * gs)
        return to_rows(re), to_rows(im)

    bf_r, bf_i = inject(0, jnp.arange(q - 1, -1, -1))
    bb_r, bb_i = inject(1, jnp.arange(q))
    tbm = jnp.concatenate([toep, bf_r, bb_r, bf_i, bb_i], axis=1).astype(BF16)

    def readout(z, pw):
        pr, pi = p_r[z][:, pw], p_i[z][:, pw]
        re = c_re[z][:, None] * pr[:, :, None] - c_im[z][:, None] * pi[:, :, None]
        im = c_re[z][:, None] * pi[:, :, None] + c_im[z][:, None] * pr[:, :, None]
        return re.reshape(ng, q * gs, ns), -im.reshape(ng, q * gs, ns)

    cf_r, cf_i = readout(0, jnp.arange(1, q + 1))
    cb_r, cb_i = readout(1, jnp.arange(q, 0, -1))
    cgm = jnp.concatenate([cf_r, cb_r, cf_i, cb_i], axis=2).astype(BF16)

    rows = []
    for lvl in range(7):
        pr, pi = power(float(q << lvl))
        rows += [v.reshape(ng // S5_GB, S5_GB * ns) for v in (pr[0], pi[0], pr[1], pi[1])]
    rows += [jnp.zeros_like(rows[0])] * 4
    sc = jnp.stack(rows, axis=1)
    ones = jnp.ones((1, S5_NCHUNK), F32)
    return {"tb": tbm, "cg": cgm, "sc": sc, "dcol": d[:, None] * ones, "wgt": w_glu.T.astype(BF16),
            "bcol": b_glu[:, None] * ones}


def _prep_ssd(a_log, dt_bias, d):
    nh = SSD_HEADS
    a = (-jnp.exp(a_log)).reshape(1, 2 * nh)
    db = dt_bias.reshape(1, 2 * nh)
    head_of_lane = jnp.arange(SSD_WIDTH) // SSD_HEADDIM
    onehot = (jnp.arange(nh)[:, None] == head_of_lane[None, :]).astype(BF16)
    zero = jnp.zeros_like(onehot)
    i = jnp.arange(SSD_CHUNK)
    return {"a": a, "at": a.T, "db": db, "dbt": db.T,
            "dexp": jnp.repeat(d, SSD_HEADDIM).reshape(1, SSD_WIDTH),
            "ef": jnp.concatenate([onehot, zero], 0), "eb": jnp.concatenate([zero, onehot], 0),
            "tril": (i[:, None] >= i[None, :]).astype(BF16), "triu": (i[:, None] <= i[None, :]).astype(BF16)}


def kernel(x, norm_mix, w_in, hy_short_w, hy_short_b, hy_w1, hy_b1, hy_freq, hy_w2, hy_b2, hy_w3, hy_decay, hy_bias, s5_a_re, s5_a_im, s5_log_dt, s5_b_re, s5_b_im, s5_c_re, s5_c_im, s5_d, s5_w_glu, s5_b_glu, ssd_conv_w, ssd_conv_b, ssd_a_log, ssd_dt_bias, ssd_d, ssd_norm, p_a, p_b, p_c, w_out, norm_ffn, ffn_up, ffn_conv_w, ffn_conv_b, ffn_down, norm_final):
    bsz, seq, dm = x.shape
    assert (seq, dm) == (SEQ, D_MODEL)
    x2 = x.reshape(bsz * seq, dm)
    fc, fs = _dft_matrices()
    nf = norm_final.reshape(1, D_MODEL)
    for l in range(DEPTH):
        pw = _prep_proj(w_in[l], hy_short_w[l], hy_short_b[l], ssd_conv_w[l], ssd_conv_b[l])
        kre, kim, knyq = _hyena_spectrum(
            _hyena_taps(hy_w1[l], hy_b1[l], hy_freq[l], hy_w2[l], hy_b2[l], hy_w3[l], hy_decay[l]), fc, fs)
        hw = {"kre": kre, "kim": kim, "knyq": knyq, "bias": hy_bias[l].reshape(1, HY_WIDTH)}
        s5w = _prep_s5(s5_a_re[l], s5_a_im[l], s5_log_dt[l], s5_b_re[l], s5_b_im[l], s5_c_re[l], s5_c_im[l],
                       s5_d[l], s5_w_glu[l], s5_b_glu[l])
        sdw = _prep_ssd(ssd_a_log[l], ssd_dt_bias[l], ssd_d[l])
        mw = {"pa": p_a[l].astype(BF16), "pb": p_b[l].astype(BF16), "pc": p_c[l].astype(BF16),
              "wo": w_out[l].astype(BF16), "nc": ssd_norm[l].reshape(1, SSD_WIDTH),
              "nw": norm_ffn[l].reshape(1, D_MODEL)}
        fw = {"up": ffn_up[l].astype(BF16), "cw": _rep8(ffn_conv_w[l]), "cb": _rep8(ffn_conv_b[l]),
              "down": ffn_down[l].astype(BF16)}

        hy_u, s5_u, z, xbc, dt, dtt, gates = _proj(x2, norm_mix[l].reshape(1, D_MODEL), pw)
        ya = _hyena(hy_u, hw, fc, fs)
        yb = _s5(s5_u, s5w)
        yc = _ssd(xbc, dt, dtt, sdw)
        x_mid, h2 = _merge(x2, ya, yb, yc, z, gates, mw)
        x2 = _ffn(h2, x_mid, fw, nf, final_norm=(l == DEPTH - 1))
    return x2.reshape(bsz, seq, dm)
```

```python
import functools
import math

import jax
import jax.numpy as jnp
from jax import lax
from jax.experimental import pallas as pl
from jax.experimental.pallas import tpu as pltpu

F32 = jnp.float32
BF16 = jnp.bfloat16

D_MODEL = 1024
SEQ = 2048
DEPTH = 2
EPS = 1e-6

HY_WIDTH = 512
HY_BANDS = 16
NFFT = 2 * SEQ

S5_WIDTH = 512
S5_GROUP = 16
S5_GROUPS = S5_WIDTH // S5_GROUP
S5_STATE = 64
S5_CHUNK = 16
S5_NCHUNK = SEQ // S5_CHUNK
S5_SLABS = S5_WIDTH // 128
S5_GB = 8

SSD_WIDTH = 1024
SSD_HEADDIM = 64
SSD_HEADS = SSD_WIDTH // SSD_HEADDIM
SSD_GROUPS = 2
SSD_STATE = 128
SSD_BC = SSD_GROUPS * SSD_STATE
SSD_XBC = SSD_WIDTH + 2 * SSD_BC
SSD_CONV = 5
SSD_CHUNK = 128
SSD_NCHUNK = SEQ // SSD_CHUNK
HEADS_PER_GROUP = SSD_HEADS // SSD_GROUPS

FFN_HIDDEN = 2816
FFN_TILE = 256
FFN_NTILE = FFN_HIDDEN // FFN_TILE
ROW_HALO = 16

PROJ_TM = 512
HY_TILE = 256
HY_FTILE = 256
VMEM_LIMIT = 56 * 1024 * 1024


def _cparams(sem):
    return pltpu.CompilerParams(dimension_semantics=sem, vmem_limit_bytes=VMEM_LIMIT)


def _const_spec(shape):
    nd = len(shape)
    return pl.BlockSpec(shape, lambda *_: (0,) * nd, pipeline_mode=pl.Buffered(1))


def _rms(x, w):
    return x * lax.rsqrt(jnp.mean(x * x, axis=-1, keepdims=True) + EPS) * w


def _dot(a, b):
    return jnp.dot(a, b, preferred_element_type=F32)


def _dot_nt(a, b):
    return lax.dot_general(a, b, (((1,), (1,)), ((), ())), preferred_element_type=F32)


def _dot_tn(a, b):
    return lax.dot_general(a, b, (((0,), (0,)), ((), ())), preferred_element_type=F32)


def _split2(a):
    a1 = a.astype(BF16)
    return a1, (a - a1.astype(F32)).astype(BF16)


def _dft_kernel(fc_ref, fs_ref, bc_scr, bs_scr):
    rows = fc_ref.shape[0]
    i = pl.program_id(0)
    w = 2.0 * math.pi / NFFT

    @pl.when(i == 0)
    def _():
        r = lax.broadcasted_iota(jnp.int32, (rows, SEQ), 0)
        c = lax.broadcasted_iota(jnp.int32, (rows, SEQ), 1)
        ang = ((r * c) & (NFFT - 1)).astype(F32) * w
        bc_scr[...] = jnp.cos(ang)
        bs_scr[...] = jnp.sin(ang)

    t8 = lax.broadcasted_iota(jnp.int32, (8, SEQ), 1)
    a = ((i * rows * t8) & (NFFT - 1)).astype(F32) * w
    ca, sa = jnp.cos(a), jnp.sin(a)
    bc = bc_scr[...].reshape(rows // 8, 8, SEQ)
    bs = bs_scr[...].reshape(rows // 8, 8, SEQ)
    fc_ref[...] = (ca * bc - sa * bs).reshape(rows, SEQ).astype(BF16)
    fs_ref[...] = (sa * bc + ca * bs).reshape(rows, SEQ).astype(BF16)


def _dft_matrices():
    rows = 256
    return pl.pallas_call(
        _dft_kernel,
        grid=(SEQ // rows,),
        out_specs=[pl.BlockSpec((rows, SEQ), lambda i: (i, 0))] * 2,
        out_shape=[jax.ShapeDtypeStruct((SEQ, SEQ), BF16)] * 2,
        scratch_shapes=[pltpu.VMEM((rows, SEQ), F32)] * 2,
        compiler_params=_cparams(("arbitrary",)),
        name="dft_matrices",
    )()


def _proj_kernel(x_ref, xp_ref, xn_ref, nw_ref, why_ref, ws5_ref, wz_ref, wxbc_ref, wdt_ref, wdtt_ref, wg_ref,
                 hcw_ref, hcb_ref, scw_ref, scb_ref,
                 hy_ref, s5_ref, z_ref, xbc_ref, dt_ref, dtt_ref, g_ref, stage_a, stage_b):
    tm = x_ref.shape[0]
    tiles_per_seq = SEQ // tm
    pos = pl.program_id(0) % tiles_per_seq
    xp = jnp.where(pos == 0, 0.0, xp_ref[...])
    xn = jnp.where(pos == tiles_per_seq - 1, 0.0, xn_ref[...])
    hext = _rms(jnp.concatenate([xp, x_ref[...], xn], axis=0), nw_ref[...]).astype(BF16)
    h = hext[ROW_HALO:ROW_HALO + tm]

    def conv(w_ref, cw_ref, cb_ref, cols, stage):
        stage[...] = _dot(hext, w_ref[:, cols])
        taps = cw_ref.shape[0]
        half = taps // 2
        acc = None
        for k in range(taps):
            term = stage[pl.ds(ROW_HALO + k - half, tm), :].reshape(tm // 8, 8, ctile) * cw_ref[k, :, cols]
            acc = term if acc is None else acc + term
        return (acc + cb_ref[:, cols]).reshape(tm, ctile)

    ctile = stage_a.shape[1]
    nconv = 3 * HY_WIDTH // ctile
    for j in range(2 * nconv):
        cols = slice(ctile * (j % nconv), ctile * (j % nconv + 1))
        stage = stage_a if j % 2 == 0 else stage_b
        if j < nconv:
            hy_ref[:, cols] = conv(why_ref, hcw_ref, hcb_ref, cols, stage).astype(BF16)
        else:
            xbc = conv(wxbc_ref, scw_ref, scb_ref, cols, stage)
            xbc_ref[:, cols] = (xbc * jax.nn.sigmoid(xbc)).astype(BF16)
        gcols = slice(ctile * j, ctile * (j + 1))
        g_ref[:, gcols] = jax.nn.sigmoid(_dot(h, wg_ref[:, gcols])).astype(BF16)
    u = _dot(h, ws5_ref[...])
    for j in range(S5_SLABS):
        s5_ref[j] = u[:, 128 * j:128 * (j + 1)]
    z = _dot(h, wz_ref[...])
    z_ref[...] = (z * jax.nn.sigmoid(z)).astype(BF16)
    dt_ref[...] = _dot(h, wdt_ref[...])
    dtt_ref[...] = _dot_nt(wdtt_ref[...], h)


def _proj(x2, nw, w):
    t = x2.shape[0]
    tm = PROJ_TM
    hb = tm // ROW_HALO
    last = t // ROW_HALO - 1
    row = lambda n: pl.BlockSpec((tm, n), lambda i: (i, 0))
    names = ("hy", "s5", "z", "xbc", "dt", "dtt", "gate", "hcw", "hcb", "scw", "scb")
    return pl.pallas_call(
        _proj_kernel,
        grid=(t // tm,),
        in_specs=[row(D_MODEL),
                  pl.BlockSpec((ROW_HALO, D_MODEL), lambda i: (jnp.maximum(i * hb - 1, 0), 0)),
                  pl.BlockSpec((ROW_HALO, D_MODEL), lambda i: (jnp.minimum((i + 1) * hb, last), 0)),
                  _const_spec((1, D_MODEL))] + [_const_spec(w[n].shape) for n in names],
        out_specs=[row(3 * HY_WIDTH),
                   pl.BlockSpec((S5_SLABS, tm, 128), lambda i: (0, i, 0)),
                   row(SSD_WIDTH), row(SSD_XBC), row(2 * SSD_HEADS),
                   pl.BlockSpec((2 * SSD_HEADS, tm), lambda i: (0, i)),
                   row(3 * D_MODEL)],
        out_shape=[jax.ShapeDtypeStruct((t, 3 * HY_WIDTH), BF16),
                   jax.ShapeDtypeStruct((S5_SLABS, t, 128), F32),
                   jax.ShapeDtypeStruct((t, SSD_WIDTH), BF16),
                   jax.ShapeDtypeStruct((t, SSD_XBC), BF16),
                   jax.ShapeDtypeStruct((t, 2 * SSD_HEADS), F32),
                   jax.ShapeDtypeStruct((2 * SSD_HEADS, t), F32),
                   jax.ShapeDtypeStruct((t, 3 * D_MODEL), BF16)],
        scratch_shapes=[pltpu.VMEM((tm + 2 * ROW_HALO, 256), F32)] * 2,
        compiler_params=_cparams(("arbitrary",)),
        name="in_proj",
    )(x2, x2, x2, nw, *[w[n] for n in names])


def _hyena_kernel(x0_ref, x1_ref, v_ref, fc_ref, fs_ref, kre_ref, kim_ref, knyq_ref, bias_ref, o_ref):
    row = lax.broadcasted_iota(jnp.int32, (SEQ, HY_TILE), 0)
    s = v_ref[...].astype(F32) * x1_ref[...].astype(F32)
    sb = s.astype(BF16)
    y = None
    for j in range(SEQ // HY_FTILE):
        fr = slice(HY_FTILE * j, HY_FTILE * (j + 1))
        sre = _dot(fc_ref[fr, :], sb)
        sim = _dot(fs_ref[fr, :], sb)
        kre = kre_ref[fr, :]
        kim = kim_ref[fr, :]
        yre = (sre * kre + sim * kim).astype(BF16)
        yim = (sim * kre - sre * kim).astype(BF16)
        part = _dot(fc_ref[:, fr], yre) + _dot(fs_ref[:, fr], yim)
        y = part if y is None else y + part
    alt = jnp.where((row & 1) == 0, 1.0, -1.0)
    snyq = jnp.sum(s * alt, axis=0, keepdims=True)
    y = y + alt * (snyq * knyq_ref[...])
    o_ref[...] = (x0_ref[...].astype(F32) * (y + bias_ref[...] * s)).astype(o_ref.dtype)


def _hyena(hy_u, hw, fc, fs):
    t = hy_u.shape[0]
    nb = t // SEQ
    nj = HY_WIDTH // HY_TILE
    part = lambda p: pl.BlockSpec((SEQ, HY_TILE), lambda j, b, p=p: (b, p * nj + j))
    ctile = lambda r: pl.BlockSpec((r, HY_TILE), lambda j, b: (0, j))
    return pl.pallas_call(
        _hyena_kernel,
        grid=(nj, nb),
        in_specs=[part(0), part(1), part(2),
                  _const_spec((SEQ, SEQ)), _const_spec((SEQ, SEQ)),
                  ctile(SEQ), ctile(SEQ), ctile(1), ctile(1)],
        out_specs=pl.BlockSpec((SEQ, HY_TILE), lambda j, b: (b, j)),
        out_shape=jax.ShapeDtypeStruct((t, HY_WIDTH), BF16),
        compiler_params=_cparams(("arbitrary", "arbitrary")),
        name="hyena",
    )(hy_u, hy_u, hy_u, fc, fs, hw["kre"], hw["kim"], hw["knyq"], hw["bias"])


def _s5_kernel(u_ref, tb_ref, cg_ref, sc_ref, dcol_ref, wgt_ref, bcol_ref, o_ref, ut_scr, yt_scr):
    nk = S5_NCHUNK
    for t in range(S5_CHUNK):
        for j in range(S5_SLABS):
            blk = u_ref[j, pl.ds(t, nk, stride=S5_CHUNK), :]
            ut_scr[t, 128 * j:128 * (j + 1), :] = blk.T

    half = S5_STATE
    gb = S5_GB
    width = gb * half
    row = lax.broadcasted_iota(jnp.int32, (nk, width), 0)

    def from_prev(x, s):
        if s % 8 == 0:
            return jnp.concatenate([jnp.zeros((s, width), F32), x[:nk - s]], axis=0)
        return jnp.where(row >= s, pltpu.roll(x, s, 0), 0.0)

    def from_next(x, s):
        if s % 8 == 0:
            return jnp.concatenate([x[s:], jnp.zeros((s, width), F32)], axis=0)
        return jnp.where(row < nk - s, pltpu.roll(x, nk - s, 0), 0.0)

    def group_block(p, carry):
        rs = []
        for i in range(gb):
            r0 = pl.multiple_of((gb * p + i) * S5_GROUP, S5_GROUP)
            ucol = ut_scr[:, pl.ds(r0, S5_GROUP), :].reshape(S5_CHUNK * S5_GROUP, nk).astype(BF16)
            rs.append(_dot(tb_ref[gb * p + i], ucol))

        def states(lo):
            return jnp.concatenate([r[lo:lo + half] for r in rs], axis=0).T

        fr, br, fi, bi = states(256), states(256 + half), states(256 + 2 * half), states(256 + 3 * half)
        sc = sc_ref[p]
        for lvl in range(7):
            s = 1 << lvl
            arf, aif, arb, aib = (sc[4 * lvl + i:4 * lvl + i + 1] for i in range(4))
            pfr, pfi = from_prev(fr, s), from_prev(fi, s)
            pbr, pbi = from_next(br, s), from_next(bi, s)
            fr, fi = fr + arf * pfr - aif * pfi, fi + arf * pfi + aif * pfr
            br, bi = br + arb * pbr - aib * pbi, bi + arb * pbi + aib * pbr
        carried = [from_prev(fr, 1).T, from_next(br, 1).T, from_prev(fi, 1).T, from_next(bi, 1).T]
        for i in range(gb):
            r0 = pl.multiple_of((gb * p + i) * S5_GROUP, S5_GROUP)
            hs = jnp.concatenate([c[half * i:half * (i + 1)] for c in carried], axis=0).astype(BF16)
            z = rs[i][0:256] + _dot(cg_ref[gb * p + i], hs)
            yt_scr[:, pl.ds(r0, S5_GROUP), :] = z.reshape(S5_CHUNK, S5_GROUP, nk)
        return carry

    lax.fori_loop(0, S5_GROUPS // gb, group_block, 0)

    for t in range(S5_CHUNK):
        gl = jax.nn.gelu(yt_scr[t] + dcol_ref[...] * ut_scr[t])
        z2 = _dot(wgt_ref[...], gl.astype(BF16)) + bcol_ref[...]
        yb = gl * jax.nn.sigmoid(z2)
        for j in range(S5_SLABS):
            o_ref[j, pl.ds(t, nk, stride=S5_CHUNK), :] = yb[128 * j:128 * (j + 1), :].T


def _s5(s5_u, sw):
    t = s5_u.shape[1]
    nb = t // SEQ
    slab = pl.BlockSpec((S5_SLABS, SEQ, 128), lambda b: (0, b, 0))
    return pl.pallas_call(
        _s5_kernel,
        grid=(nb,),
        in_specs=[slab, _const_spec(sw["tb"].shape), _const_spec(sw["cg"].shape), _const_spec(sw["sc"].shape),
                  _const_spec(sw["dcol"].shape), _const_spec(sw["wgt"].shape), _const_spec(sw["bcol"].shape)],
        out_specs=slab,
        out_shape=jax.ShapeDtypeStruct((S5_SLABS, t, 128), F32),
        scratch_shapes=[pltpu.VMEM((S5_CHUNK, S5_WIDTH, S5_NCHUNK), F32),
                        pltpu.VMEM((S5_CHUNK, S5_WIDTH, S5_NCHUNK), F32)],
        compiler_params=_cparams(("arbitrary",)),
        name="s5",
    )(s5_u, sw["tb"], sw["cg"], sw["sc"], sw["dcol"], sw["wgt"], sw["bcol"])


def _ssd_kernel(xs_scr, bc_scr, dt_ref, dtt_ref, a_ref, at_ref, db_ref, dbt_ref, dexp_ref,
                ef_ref, eb_ref, tril_ref, triu_ref, o_ref,
                y_scr, carry_scr, dt_scr, a_scr, dtt_scr, att_scr, cs_scr):
    q = SSD_CHUNK
    nh = SSD_HEADS

    dtv = jax.nn.softplus(dt_ref[...] + db_ref[...])
    dt_scr[...] = dtv
    a_scr[...] = dtv * a_ref[...]
    dtvt = jax.nn.softplus(dtt_ref[...] + dbt_ref[...])
    avt = dtvt * at_ref[...]
    for c in range(SSD_NCHUNK):
        dtt_scr[c] = dtvt[:, q * c:q * (c + 1)]
        att_scr[c] = avt[:, q * c:q * (c + 1)]

    ri = lax.broadcasted_iota(jnp.int32, (q, q), 0)
    ci = lax.broadcasted_iota(jnp.int32, (q, q), 1)
    causal = ri >= ci
    anti = ci >= ri
    col_is_f = lax.broadcasted_iota(jnp.int32, (q, 2 * nh), 1) < nh
    row_is_f = lax.broadcasted_iota(jnp.int32, (2 * nh, q), 0) < nh
    lane_lo = lax.broadcasted_iota(jnp.int32, (q, 128), 1) < SSD_HEADDIM
    tril = tril_ref[...]
    triu = triu_ref[...]

    split2 = _split2

    def col_cumsum(a_c):
        parts = split2(a_c)
        lo = sum(_dot(tril, p) for p in parts)
        hi = sum(_dot(triu, p) for p in parts)
        return jnp.where(col_is_f, lo, hi)

    def row_cumsum(at_c):
        parts = split2(at_c)
        lo = sum(_dot(p, triu) for p in parts)
        hi = sum(_dot(p, tril) for p in parts)
        return jnp.where(row_is_f, lo, hi)

    def expand(v, e_ref):
        return sum(_dot(p, e_ref[...]) for p in split2(v))

    def inter_chunk(c, cs, e_ref):
        r0 = pl.multiple_of(c * q, q)
        total = jnp.where(col_is_f[0:1], cs[q - 1:q], cs[0:1])
        w_state = _dot((dt_scr[pl.ds(r0, q), :] * jnp.exp(total - cs)).astype(BF16), e_ref[...])
        w_in = _dot(jnp.exp(cs).astype(BF16), e_ref[...])
        decay = expand(jnp.broadcast_to(jnp.exp(total), (8, 2 * nh)), e_ref)[0:1]
        xdd = (xs_scr[pl.ds(r0, q), :] * w_state).astype(BF16)
        outs = []
        for g in range(SSD_GROUPS):
            bg = bc_scr[pl.ds(r0, q), SSD_STATE * g:SSD_STATE * (g + 1)]
            cg = bc_scr[pl.ds(r0, q), SSD_BC + SSD_STATE * g:SSD_BC + SSD_STATE * (g + 1)]
            lanes = slice(512 * g, 512 * (g + 1))
            prev = carry_scr[g]
            outs.append(_dot(cg, prev.astype(BF16)) * w_in[:, lanes])
            carry_scr[g] = prev * decay[:, lanes] + _dot_tn(bg, xdd[:, lanes])
        return r0, jnp.concatenate(outs, axis=1)

    def forward_chunk(c, carry):
        r0 = pl.multiple_of(c * q, q)
        cs = col_cumsum(a_scr[pl.ds(r0, q), :])
        cs_scr[pl.ds(r0, q), :] = cs
        rs = row_cumsum(att_scr[c])
        dtt_c = dtt_scr[c]
        xs_b = xs_scr[pl.ds(r0, q), :]
        pairs = []
        for g in range(SSD_GROUPS):
            bg = bc_scr[pl.ds(r0, q), SSD_STATE * g:SSD_STATE * (g + 1)]
            cg = bc_scr[pl.ds(r0, q), SSD_BC + SSD_STATE * g:SSD_BC + SSD_STATE * (g + 1)]
            scores = _dot_nt(cg, bg)
            for j in range(HEADS_PER_GROUP // 2):
                ms = []
                for h in (HEADS_PER_GROUP * g + 2 * j, HEADS_PER_GROUP * g + 2 * j + 1):
                    hb = nh + h
                    lf = jnp.where(causal, jnp.exp(cs[:, h:h + 1] - rs[h:h + 1, :]), 0.0) * dtt_c[h:h + 1, :]
                    lb = jnp.where(anti, jnp.exp(cs[:, hb:hb + 1] - rs[hb:hb + 1, :]), 0.0) * dtt_c[hb:hb + 1, :]
                    ms.append((scores * (lf + lb)).astype(BF16))
                blk = HEADS_PER_GROUP // 2 * g + j
                xp = xs_b[:, 128 * blk:128 * (blk + 1)]
                rhs = jnp.concatenate([jnp.where(lane_lo, xp, 0), jnp.where(lane_lo, 0, xp)], axis=0)
                pairs.append(_dot(jnp.concatenate(ms, axis=1), rhs))
        _, y_off = inter_chunk(c, cs, ef_ref)
        y_scr[pl.ds(r0, q), :] = jnp.concatenate(pairs, axis=1) + y_off
        return carry

    def backward_chunk(i, carry):
        c = SSD_NCHUNK - 1 - i
        r0, y_off = inter_chunk(c, cs_scr[pl.ds(pl.multiple_of(c * q, q), q), :], eb_ref)
        y_scr[pl.ds(r0, q), :] += y_off
        return carry

    carry_scr[...] = jnp.zeros_like(carry_scr)
    lax.fori_loop(0, SSD_NCHUNK, forward_chunk, 0, unroll=4)
    carry_scr[...] = jnp.zeros_like(carry_scr)
    lax.fori_loop(0, SSD_NCHUNK, backward_chunk, 0, unroll=2)

    rows = 256
    for i in range(SEQ // rows):
        sl = slice(rows * i, rows * (i + 1))
        o_ref[sl, :] = (y_scr[sl, :] + dexp_ref[...] * xs_scr[sl, :]).astype(o_ref.dtype)


def _ssd(xbc, dt, dtt, sw):
    t = xbc.shape[0]
    nb = t // SEQ
    row = lambda n: pl.BlockSpec((SEQ, n), lambda b: (b, 0))
    names = ("a", "at", "db", "dbt", "dexp", "ef", "eb", "tril", "triu")
    return pl.pallas_call(
        _ssd_kernel,
        grid=(nb,),
        in_specs=[row(SSD_WIDTH),
                  pl.BlockSpec((SEQ, 2 * SSD_BC), lambda b: (b, SSD_WIDTH // (2 * SSD_BC))),
                  row(2 * SSD_HEADS),
                  pl.BlockSpec((2 * SSD_HEADS, SEQ), lambda b: (0, b))]
                 + [_const_spec(sw[n].shape) for n in names],
        out_specs=row(SSD_WIDTH),
        out_shape=jax.ShapeDtypeStruct((t, SSD_WIDTH), BF16),
        scratch_shapes=[pltpu.VMEM((SEQ, SSD_WIDTH), F32),
                        pltpu.VMEM((SSD_GROUPS, SSD_STATE, SSD_WIDTH // SSD_GROUPS), F32),
                        pltpu.VMEM((SEQ, 2 * SSD_HEADS), F32),
                        pltpu.VMEM((SEQ, 2 * SSD_HEADS), F32),
                        pltpu.VMEM((SSD_NCHUNK, 2 * SSD_HEADS, SSD_CHUNK), F32),
                        pltpu.VMEM((SSD_NCHUNK, 2 * SSD_HEADS, SSD_CHUNK), F32),
                        pltpu.VMEM((SEQ, 2 * SSD_HEADS), F32)],
        compiler_params=_cparams(("arbitrary",)),
        name="ssd",
    )(xbc, xbc, dt, dtt, *[sw[n] for n in names])


def _merge_kernel(x_ref, ya_ref, yb_ref, yc_ref, z_ref, g_ref, pa_ref, pb_ref, pc_ref, wo_ref, nc_ref, nw_ref,
                  xo_ref, h_ref):
    gate = g_ref[...].astype(F32)
    yb = jnp.concatenate([yb_ref[j] for j in range(S5_SLABS)], axis=1).astype(BF16)
    yc = _rms(yc_ref[...].astype(F32) * z_ref[...].astype(F32), nc_ref[...]).astype(BF16)
    merged = (gate[:, 0:D_MODEL] * _dot(ya_ref[...], pa_ref[...])
              + gate[:, D_MODEL:2 * D_MODEL] * _dot(yb, pb_ref[...])
              + gate[:, 2 * D_MODEL:] * _dot(yc, pc_ref[...]))
    xn = x_ref[...] + _dot(merged.astype(BF16), wo_ref[...])
    xo_ref[...] = xn
    h_ref[...] = _rms(xn, nw_ref[...]).astype(BF16)


def _merge(x2, ya, yb, yc, z, gates, mw):
    t = x2.shape[0]
    tm = PROJ_TM
    row = lambda n: pl.BlockSpec((tm, n), lambda i: (i, 0))
    return pl.pallas_call(
        _merge_kernel,
        grid=(t // tm,),
        in_specs=[row(D_MODEL), row(HY_WIDTH), pl.BlockSpec((S5_SLABS, tm, 128), lambda i: (0, i, 0)),
                  row(SSD_WIDTH), row(SSD_WIDTH), row(3 * D_MODEL),
                  _const_spec(mw["pa"].shape), _const_spec(mw["pb"].shape), _const_spec(mw["pc"].shape),
                  _const_spec(mw["wo"].shape), _const_spec((1, SSD_WIDTH)), _const_spec((1, D_MODEL))],
        out_specs=[row(D_MODEL), row(D_MODEL)],
        out_shape=[jax.ShapeDtypeStruct((t, D_MODEL), F32), jax.ShapeDtypeStruct((t, D_MODEL), BF16)],
        compiler_params=_cparams(("arbitrary",)),
        name="merge",
    )(x2, ya, yb, yc, z, gates, mw["pa"], mw["pb"], mw["pc"], mw["wo"], mw["nc"], mw["nw"])


def _ffn_kernel(h_ref, hp_ref, hn_ref, x_ref, wu_ref, cw_ref, cb_ref, wd_ref, nf_ref, o_ref, stage_a, stage_b,
                *, final_norm):
    tm = h_ref.shape[0]
    tiles_per_seq = SEQ // tm
    pos = pl.program_id(0) % tiles_per_seq
    prev = jnp.where(pos == 0, jnp.zeros_like(hp_ref[...]), hp_ref[...])
    nxt = jnp.where(pos == tiles_per_seq - 1, jnp.zeros_like(hn_ref[...]), hn_ref[...])
    hext = jnp.concatenate([prev, h_ref[...], nxt], axis=0)

    def conv(cols, stage):
        stage[...] = _dot(hext, wu_ref[:, cols])
        acc = None
        for k in range(3):
            term = stage[pl.ds(ROW_HALO + k - 1, tm), :].reshape(tm // 8, 8, FFN_TILE) * cw_ref[k, :, cols]
            acc = term if acc is None else acc + term
        return (acc + cb_ref[:, cols]).reshape(tm, FFN_TILE)

    acts = []
    for j in range(FFN_NTILE):
        gate = conv(slice(FFN_TILE * j, FFN_TILE * (j + 1)), stage_a)
        val = conv(slice(FFN_HIDDEN + FFN_TILE * j, FFN_HIDDEN + FFN_TILE * (j + 1)), stage_b)
        acts.append((gate * jax.nn.sigmoid(gate) * val).astype(BF16))
    y = x_ref[...] + _dot(jnp.concatenate(acts, axis=1), wd_ref[...])
    o_ref[...] = _rms(y, nf_ref[...]) if final_norm else y


def _ffn(h2, x2, fw, nf, final_norm):
    t = x2.shape[0]
    tm = PROJ_TM
    hb = tm // ROW_HALO
    last = t // ROW_HALO - 1
    row = lambda n: pl.BlockSpec((tm, n), lambda i: (i, 0))
    return pl.pallas_call(
        functools.partial(_ffn_kernel, final_norm=final_norm),
        grid=(t // tm,),
        in_specs=[row(D_MODEL),
                  pl.BlockSpec((ROW_HALO, D_MODEL), lambda i: (jnp.maximum(i * hb - 1, 0), 0)),
                  pl.BlockSpec((ROW_HALO, D_MODEL), lambda i: (jnp.minimum((i + 1) * hb, last), 0)),
                  row(D_MODEL),
                  _const_spec(fw["up"].shape), _const_spec(fw["cw"].shape), _const_spec(fw["cb"].shape),
                  _const_spec(fw["down"].shape), _const_spec((1, D_MODEL))],
        out_specs=row(D_MODEL),
        out_shape=jax.ShapeDtypeStruct((t, D_MODEL), F32),
        scratch_shapes=[pltpu.VMEM((tm + 2 * ROW_HALO, FFN_TILE), F32)] * 2,
        compiler_params=_cparams(("arbitrary",)),
        name="conv_ffn",
    )(h2, h2, h2, x2, fw["up"], fw["cw"], fw["cb"], fw["down"], nf)


_HI = lax.Precision.HIGHEST


def _rep8(v):
    return jnp.broadcast_to(v[..., None, :], v.shape[:-1] + (8, v.shape[-1]))


def _prep_proj(w_in, hy_short_w, hy_short_b, ssd_conv_w, ssd_conv_b):
    o = [0, 3 * HY_WIDTH, 3 * HY_WIDTH + S5_WIDTH]
    o.append(o[-1] + SSD_WIDTH)
    o.append(o[-1] + SSD_XBC)
    o.append(o[-1] + 2 * SSD_HEADS)
    cut = lambda a, b: w_in[:, a:b].astype(BF16)
    return {"hy": cut(o[0], o[1]), "s5": cut(o[1], o[2]), "z": cut(o[2], o[3]), "xbc": cut(o[3], o[4]),
            "dt": cut(o[4], o[5]), "dtt": cut(o[4], o[5]).T, "gate": cut(o[5], w_in.shape[1]),
            "hcw": _rep8(hy_short_w), "hcb": _rep8(hy_short_b), "scw": _rep8(ssd_conv_w), "scb": _rep8(ssd_conv_b)}


def _hyena_taps(w1, b1, freq, w2, b2, w3, decay):
    pos = jnp.arange(SEQ, dtype=F32)
    tt = (pos / max(SEQ - 1, 1))[:, None]
    bands = jnp.linspace(1e-4, HY_BANDS - 1, HY_BANDS, dtype=F32)
    ang = (2.0 * math.pi / SEQ) * pos[:, None] * bands[None, :]
    zz = jnp.concatenate([tt, jnp.cos(ang), -jnp.sin(ang)], axis=-1)
    h = jnp.sin(freq[0] * (jnp.dot(zz, w1, precision=_HI) + b1))
    h = jnp.sin(freq[1] * (jnp.dot(h, w2, precision=_HI) + b2))
    return jnp.dot(h, w3, precision=_HI) * jnp.exp(-tt * jnp.abs(decay))


def _spectrum_kernel(hf_ref, hb_ref, fc_ref, fs_ref, kre_ref, kim_ref, knyq_ref):
    row = lax.broadcasted_iota(jnp.int32, (SEQ, HY_TILE), 0)
    hf = hf_ref[...]
    hb = jnp.where(row == 0, 0.0, hb_ref[...])
    inv = 1.0 / (NFFT * (jnp.sum(jnp.abs(hf), axis=0, keepdims=True) + jnp.sum(jnp.abs(hb), axis=0, keepdims=True)))
    even = hf + hb
    odd = hb - hf
    dft = lambda m_ref, v: sum(_dot(m_ref[...], p) for p in _split2(v))
    freq = lax.broadcasted_iota(jnp.int32, (HY_FTILE, HY_TILE), 0) + pl.program_id(1) * HY_FTILE
    scale = jnp.where(freq == 0, 1.0, 2.0) * inv
    kre_ref[...] = dft(fc_ref, even) * scale
    kim_ref[...] = dft(fs_ref, odd) * scale
    alt = jnp.where((row & 1) == 0, 1.0, -1.0)
    knyq_ref[...] = jnp.sum(even * alt, axis=0, keepdims=True) * inv


def _hyena_spectrum(taps, fc, fs):
    nj = HY_WIDTH // HY_TILE
    col = lambda r, off: pl.BlockSpec((r, HY_TILE), lambda j, f, off=off: (0, off + j))
    frow = pl.BlockSpec((HY_FTILE, SEQ), lambda j, f: (f, 0))
    fout = pl.BlockSpec((HY_FTILE, HY_TILE), lambda j, f: (f, j))
    return pl.pallas_call(
        _spectrum_kernel,
        grid=(nj, SEQ // HY_FTILE),
        in_specs=[col(SEQ, 0), col(SEQ, nj), frow, frow],
        out_specs=[fout, fout, col(1, 0)],
        out_shape=[jax.ShapeDtypeStruct((SEQ, HY_WIDTH), F32), jax.ShapeDtypeStruct((SEQ, HY_WIDTH), F32),
                   jax.ShapeDtypeStruct((1, HY_WIDTH), F32)],
        compiler_params=_cparams(("arbitrary", "arbitrary")),
        name="hyena_spectrum",
    )(taps, taps, fc, fs)


def _prep_s5(a_re, a_im, log_dt, b_re, b_im, c_re, c_im, d, w_glu, b_glu):
    q, gs, ns, ng = S5_CHUNK, S5_GROUP, S5_STATE, S5_GROUPS
    dt = jnp.exp(log_dt)[:, :, None]
    mag = jnp.exp(a_re * dt)
    ab_r, ab_i = mag * jnp.cos(a_im * dt), mag * jnp.sin(a_im * dt)
    den = a_re * a_re + a_im * a_im
    f_r = ((ab_r - 1.0) * a_re + ab_i * a_im) / den
    f_i = (ab_i * a_re - (ab_r - 1.0) * a_im) / den
    fb_r = f_r[..., None] * b_re - f_i[..., None] * b_im
    fb_i = f_r[..., None] * b_im + f_i[..., None] * b_re

    def power(j):
        m = jnp.exp(j * a_re * dt)
        return m * jnp.cos(j * a_im * dt), m * jnp.sin(j * a_im * dt)

    p_r, p_i = zip(*[power(float(j)) for j in range(q + 1)])
    p_r, p_i = jnp.stack(p_r, 2), jnp.stack(p_i, 2)

    cp_r = c_re[:, :, None] * p_r[:, :, :q, None] - c_im[:, :, None] * p_i[:, :, :q, None]
    cp_i = c_re[:, :, None] * p_i[:, :, :q, None] + c_im[:, :, None] * p_r[:, :, :q, None]
    kd = (jnp.einsum("zgdcn,zgne->zgdce", cp_r, fb_r, precision=_HI)
          - jnp.einsum("zgdcn,zgne->zgdce", cp_i, fb_i, precision=_HI))
    lag = jnp.arange(q)[:, None] - jnp.arange(q)[None, :]
    dd = jnp.arange(q)[:, None, None]
    sel_f = (lag[None] == dd).astype(F32)
    sel_b = (-lag[None] == dd).astype(F32)
    toep = (jnp.einsum("dtu,gdce->gtcue", sel_f, kd[0], precision=_HI)
            + jnp.einsum("dtu,gdce->gtcue", sel_b, kd[1], precision=_HI)).reshape(ng, q * gs, q * gs)

    def inject(z, pw):
        pr, pi = p_r[z][:, pw], p_i[z][:, pw]
        re = pr[..., None] * fb_r[z][:, None] - pi[..., None] * fb_i[z][:, None]
        im = pr[..., None] * fb_i[z][:, None] + pi[..., None] * fb_r[z][:, None]
        to_rows = lambda v: v.transpose(0, 2, 1, 3).reshape(ng, ns, q * gs)
        return to_rows(re), to_rows(im)

    bf_r, bf_i = inject(0, jnp.arange(q - 1, -1, -1))
    bb_r, bb_i = inject(1, jnp.arange(q))
    tbm = jnp.concatenate([toep, bf_r, bb_r, bf_i, bb_i], axis=1).astype(BF16)

    def readout(z, pw):
        pr, pi = p_r[z][:, pw], p_i[z][:, pw]
        re = c_re[z][:, None] * pr[:, :, None] - c_im[z][:, None] * pi[:, :, None]
        im = c_re[z][:, None] * pi[:, :, None] + c_im[z][:, None] * pr[:, :, None]
        return re.reshape(ng, q * gs, ns), -im.reshape(ng, q * gs, ns)

    cf_r, cf_i = readout(0, jnp.arange(1, q + 1))
    cb_r, cb_i = readout(1, jnp.arange(q, 0, -1))
    cgm = jnp.concatenate([cf_r, cb_r, cf_i, cb_i], axis=2).astype(BF16)

    rows = []
    for lvl in range(7):
        pr, pi = power(float(q << lvl))
        rows += [v.reshape(ng // S5_GB, S5_GB * ns) for v in (pr[0], pi[0], pr[1], pi[1])]
    rows += [jnp.zeros_like(rows[0])] * 4
    sc = jnp.stack(rows, axis=1)
    ones = jnp.ones((1, S5_NCHUNK), F32)
    return {"tb": tbm, "cg": cgm, "sc": sc, "dcol": d[:, None] * ones, "wgt": w_glu.T.astype(BF16),
            "bcol": b_glu[:, None] * ones}


def _prep_ssd(a_log, dt_bias, d):
    nh = SSD_HEADS
    a = (-jnp.exp(a_log)).reshape(1, 2 * nh)
    db = dt_bias.reshape(1, 2 * nh)
    head_of_lane = jnp.arange(SSD_WIDTH) // SSD_HEADDIM
    onehot = (jnp.arange(nh)[:, None] == head_of_lane[None, :]).astype(BF16)
    zero = jnp.zeros_like(onehot)
    i = jnp.arange(SSD_CHUNK)
    return {"a": a, "at": a.T, "db": db, "dbt": db.T,
            "dexp": jnp.repeat(d, SSD_HEADDIM).reshape(1, SSD_WIDTH),
            "ef": jnp.concatenate([onehot, zero], 0), "eb": jnp.concatenate([zero, onehot], 0),
            "tril": (i[:, None] >= i[None, :]).astype(BF16), "triu": (i[:, None] <= i[None, :]).astype(BF16)}


def kernel(x, norm_mix, w_in, hy_short_w, hy_short_b, hy_w1, hy_b1, hy_freq, hy_w2, hy_b2, hy_w3, hy_decay, hy_bias, s5_a_re, s5_a_im, s5_log_dt, s5_b_re, s5_b_im, s5_c_re, s5_c_im, s5_d, s5_w_glu, s5_b_glu, ssd_conv_w, ssd_conv_b, ssd_a_log, ssd_dt_bias, ssd_d, ssd_norm, p_a, p_b, p_c, w_out, norm_ffn, ffn_up, ffn_conv_w, ffn_conv_b, ffn_down, norm_final):
    bsz, seq, dm = x.shape
    assert (seq, dm) == (SEQ, D_MODEL)
    x2 = x.reshape(bsz * seq, dm)
    fc, fs = _dft_matrices()
    nf = norm_final.reshape(1, D_MODEL)
    for l in range(DEPTH):
        pw = _prep_proj(w_in[l], hy_short_w[l], hy_short_b[l], ssd_conv_w[l], ssd_conv_b[l])
        kre, kim, knyq = _hyena_spectrum(
            _hyena_taps(hy_w1[l], hy_b1[l], hy_freq[l], hy_w2[l], hy_b2[l], hy_w3[l], hy_decay[l]), fc, fs)
        hw = {"kre": kre, "kim": kim, "knyq": knyq, "bias": hy_bias[l].reshape(1, HY_WIDTH)}
        s5w = _prep_s5(s5_a_re[l], s5_a_im[l], s5_log_dt[l], s5_b_re[l], s5_b_im[l], s5_c_re[l], s5_c_im[l],
                       s5_d[l], s5_w_glu[l], s5_b_glu[l])
        sdw = _prep_ssd(ssd_a_log[l], ssd_dt_bias[l], ssd_d[l])
        mw = {"pa": p_a[l].astype(BF16), "pb": p_b[l].astype(BF16), "pc": p_c[l].astype(BF16),
              "wo": w_out[l].astype(BF16), "nc": ssd_norm[l].reshape(1, SSD_WIDTH),
              "nw": norm_ffn[l].reshape(1, D_MODEL)}
        fw = {"up": ffn_up[l].astype(BF16), "cw": _rep8(ffn_conv_w[l]), "cb": _rep8(ffn_conv_b[l]),
              "down": ffn_down[l].astype(BF16)}

        hy_u, s5_u, z, xbc, dt, dtt, gates = _proj(x2, norm_mix[l].reshape(1, D_MODEL), pw)
        ya = _hyena(hy_u, hw, fc, fs)
        yb = _s5(s5_u, s5w)
        yc = _ssd(xbc, dt, dtt, sdw)
        x_mid, h2 = _merge(x2, ya, yb, yc, z, gates, mw)
        x2 = _ffn(h2, x_mid, fw, nf, final_norm=(l == DEPTH - 1))
    return x2.reshape(bsz, seq, dm)
```

```python
import functools
import math

import jax
import jax.numpy as jnp
from jax import lax
from jax.experimental import pallas as pl
from jax.experimental.pallas import tpu as pltpu

F32 = jnp.float32
BF16 = jnp.bfloat16

D_MODEL = 1024
SEQ = 2048
DEPTH = 2
EPS = 1e-6

HY_WIDTH = 512
HY_BANDS = 16
NFFT = 2 * SEQ

S5_WIDTH = 512
S5_GROUP = 16
S5_GROUPS = S5_WIDTH // S5_GROUP
S5_STATE = 64
S5_CHUNK = 16
S5_NCHUNK = SEQ // S5_CHUNK
S5_SLABS = S5_WIDTH // 128
S5_GB = 16

SSD_WIDTH = 1024
SSD_HEADDIM = 64
SSD_HEADS = SSD_WIDTH // SSD_HEADDIM
SSD_GROUPS = 2
SSD_STATE = 128
SSD_BC = SSD_GROUPS * SSD_STATE
SSD_XBC = SSD_WIDTH + 2 * SSD_BC
SSD_CONV = 5
SSD_CHUNK = 128
SSD_NCHUNK = SEQ // SSD_CHUNK
HEADS_PER_GROUP = SSD_HEADS // SSD_GROUPS

FFN_HIDDEN = 2816
FFN_TILE = 256
FFN_NTILE = FFN_HIDDEN // FFN_TILE
ROW_HALO = 16

PROJ_TM = 512
HY_TILE = 256
HY_FTILE = 256
VMEM_LIMIT = 56 * 1024 * 1024


def _cparams(sem):
    return pltpu.CompilerParams(dimension_semantics=sem, vmem_limit_bytes=VMEM_LIMIT)


def _const_spec(shape):
    nd = len(shape)
    return pl.BlockSpec(shape, lambda *_: (0,) * nd, pipeline_mode=pl.Buffered(1))


def _rms(x, w):
    return x * lax.rsqrt(jnp.mean(x * x, axis=-1, keepdims=True) + EPS) * w


def _dot(a, b):
    return jnp.dot(a, b, preferred_element_type=F32)


def _dot_nt(a, b):
    return lax.dot_general(a, b, (((1,), (1,)), ((), ())), preferred_element_type=F32)


def _dot_tn(a, b):
    return lax.dot_general(a, b, (((0,), (0,)), ((), ())), preferred_element_type=F32)


def _split2(a):
    a1 = a.astype(BF16)
    return a1, (a - a1.astype(F32)).astype(BF16)


def _dft_kernel(fc_ref, fs_ref, bc_scr, bs_scr):
    rows = fc_ref.shape[0]
    i = pl.program_id(0)
    w = 2.0 * math.pi / NFFT

    @pl.when(i == 0)
    def _():
        r = lax.broadcasted_iota(jnp.int32, (rows, SEQ), 0)
        c = lax.broadcasted_iota(jnp.int32, (rows, SEQ), 1)
        ang = ((r * c) & (NFFT - 1)).astype(F32) * w
        bc_scr[...] = jnp.cos(ang)
        bs_scr[...] = jnp.sin(ang)

    t8 = lax.broadcasted_iota(jnp.int32, (8, SEQ), 1)
    a = ((i * rows * t8) & (NFFT - 1)).astype(F32) * w
    ca, sa = jnp.cos(a), jnp.sin(a)
    bc = bc_scr[...].reshape(rows // 8, 8, SEQ)
    bs = bs_scr[...].reshape(rows // 8, 8, SEQ)
    fc_ref[...] = (ca * bc - sa * bs).reshape(rows, SEQ).astype(BF16)
    fs_ref[...] = (sa * bc + ca * bs).reshape(rows, SEQ).astype(BF16)


def _dft_matrices():
    rows = 256
    return pl.pallas_call(
        _dft_kernel,
        grid=(SEQ // rows,),
        out_specs=[pl.BlockSpec((rows, SEQ), lambda i: (i, 0))] * 2,
        out_shape=[jax.ShapeDtypeStruct((SEQ, SEQ), BF16)] * 2,
        scratch_shapes=[pltpu.VMEM((rows, SEQ), F32)] * 2,
        compiler_params=_cparams(("arbitrary",)),
        name="dft_matrices",
    )()


def _proj_kernel(x_ref, xp_ref, xn_ref, nw_ref, why_ref, ws5_ref, wz_ref, wxbc_ref, wdt_ref, wdtt_ref, wg_ref,
                 hcw_ref, hcb_ref, scw_ref, scb_ref,
                 hy_ref, s5_ref, z_ref, xbc_ref, dt_ref, dtt_ref, g_ref, stage_a, stage_b):
    tm = x_ref.shape[0]
    tiles_per_seq = SEQ // tm
    pos = pl.program_id(0) % tiles_per_seq
    xp = jnp.where(pos == 0, 0.0, xp_ref[...])
    xn = jnp.where(pos == tiles_per_seq - 1, 0.0, xn_ref[...])
    hext = _rms(jnp.concatenate([xp, x_ref[...], xn], axis=0), nw_ref[...]).astype(BF16)
    h = hext[ROW_HALO:ROW_HALO + tm]

    def conv(w_ref, cw_ref, cb_ref, cols, stage):
        stage[...] = _dot(hext, w_ref[:, cols])
        taps = cw_ref.shape[0]
        half = taps // 2
        acc = None
        for k in range(taps):
            term = stage[pl.ds(ROW_HALO + k - half, tm), :].reshape(tm // 8, 8, ctile) * cw_ref[k, :, cols]
            acc = term if acc is None else acc + term
        return (acc + cb_ref[:, cols]).reshape(tm, ctile)

    ctile = stage_a.shape[1]
    nconv = 3 * HY_WIDTH // ctile
    for j in range(2 * nconv):
        cols = slice(ctile * (j % nconv), ctile * (j % nconv + 1))
        stage = stage_a if j % 2 == 0 else stage_b
        if j < nconv:
            hy_ref[:, cols] = conv(why_ref, hcw_ref, hcb_ref, cols, stage).astype(BF16)
        else:
            xbc = conv(wxbc_ref, scw_ref, scb_ref, cols, stage)
            xbc_ref[:, cols] = (xbc * jax.nn.sigmoid(xbc)).astype(BF16)
        gcols = slice(ctile * j, ctile * (j + 1))
        g_ref[:, gcols] = jax.nn.sigmoid(_dot(h, wg_ref[:, gcols])).astype(BF16)
    u = _dot(h, ws5_ref[...])
    for j in range(S5_SLABS):
        s5_ref[j] = u[:, 128 * j:128 * (j + 1)]
    z = _dot(h, wz_ref[...])
    z_ref[...] = (z * jax.nn.sigmoid(z)).astype(BF16)
    dt_ref[...] = _dot(h, wdt_ref[...])
    dtt_ref[...] = _dot_nt(wdtt_ref[...], h)


def _proj(x2, nw, w):
    t = x2.shape[0]
    tm = PROJ_TM
    hb = tm // ROW_HALO
    last = t // ROW_HALO - 1
    row = lambda n: pl.BlockSpec((tm, n), lambda i: (i, 0))
    names = ("hy", "s5", "z", "xbc", "dt", "dtt", "gate", "hcw", "hcb", "scw", "scb")
    return pl.pallas_call(
        _proj_kernel,
        grid=(t // tm,),
        in_specs=[row(D_MODEL),
                  pl.BlockSpec((ROW_HALO, D_MODEL), lambda i: (jnp.maximum(i * hb - 1, 0), 0)),
                  pl.BlockSpec((ROW_HALO, D_MODEL), lambda i: (jnp.minimum((i + 1) * hb, last), 0)),
                  _const_spec((1, D_MODEL))] + [_const_spec(w[n].shape) for n in names],
        out_specs=[row(3 * HY_WIDTH),
                   pl.BlockSpec((S5_SLABS, tm, 128), lambda i: (0, i, 0)),
                   row(SSD_WIDTH), row(SSD_XBC), row(2 * SSD_HEADS),
                   pl.BlockSpec((2 * SSD_HEADS, tm), lambda i: (0, i)),
                   row(3 * D_MODEL)],
        out_shape=[jax.ShapeDtypeStruct((t, 3 * HY_WIDTH), BF16),
                   jax.ShapeDtypeStruct((S5_SLABS, t, 128), F32),
                   jax.ShapeDtypeStruct((t, SSD_WIDTH), BF16),
                   jax.ShapeDtypeStruct((t, SSD_XBC), BF16),
                   jax.ShapeDtypeStruct((t, 2 * SSD_HEADS), F32),
                   jax.ShapeDtypeStruct((2 * SSD_HEADS, t), F32),
                   jax.ShapeDtypeStruct((t, 3 * D_MODEL), BF16)],
        scratch_shapes=[pltpu.VMEM((tm + 2 * ROW_HALO, 256), F32)] * 2,
        compiler_params=_cparams(("arbitrary",)),
        name="in_proj",
    )(x2, x2, x2, nw, *[w[n] for n in names])


def _hyena_kernel(x0_ref, x1_ref, v_ref, fc_ref, fs_ref, kre_ref, kim_ref, knyq_ref, bias_ref, o_ref):
    row = lax.broadcasted_iota(jnp.int32, (SEQ, HY_TILE), 0)
    s = v_ref[...].astype(F32) * x1_ref[...].astype(F32)
    sb = s.astype(BF16)
    y = None
    for j in range(SEQ // HY_FTILE):
        fr = slice(HY_FTILE * j, HY_FTILE * (j + 1))
        sre = _dot(fc_ref[fr, :], sb)
        sim = _dot(fs_ref[fr, :], sb)
        kre = kre_ref[fr, :]
        kim = kim_ref[fr, :]
        yre = (sre * kre + sim * kim).astype(BF16)
        yim = (sim * kre - sre * kim).astype(BF16)
        part = _dot(fc_ref[:, fr], yre) + _dot(fs_ref[:, fr], yim)
        y = part if y is None else y + part
    alt = jnp.where((row & 1) == 0, 1.0, -1.0)
    snyq = jnp.sum(s * alt, axis=0, keepdims=True)
    y = y + alt * (snyq * knyq_ref[...])
    o_ref[...] = (x0_ref[...].astype(F32) * (y + bias_ref[...] * s)).astype(o_ref.dtype)


def _hyena(hy_u, hw, fc, fs):
    t = hy_u.shape[0]
    nb = t // SEQ
    nj = HY_WIDTH // HY_TILE
    part = lambda p: pl.BlockSpec((SEQ, HY_TILE), lambda j, b, p=p: (b, p * nj + j))
    ctile = lambda r: pl.BlockSpec((r, HY_TILE), lambda j, b: (0, j))
    return pl.pallas_call(
        _hyena_kernel,
        grid=(nj, nb),
        in_specs=[part(0), part(1), part(2),
                  _const_spec((SEQ, SEQ)), _const_spec((SEQ, SEQ)),
                  ctile(SEQ), ctile(SEQ), ctile(1), ctile(1)],
        out_specs=pl.BlockSpec((SEQ, HY_TILE), lambda j, b: (b, j)),
        out_shape=jax.ShapeDtypeStruct((t, HY_WIDTH), BF16),
        compiler_params=_cparams(("arbitrary", "arbitrary")),
        name="hyena",
    )(hy_u, hy_u, hy_u, fc, fs, hw["kre"], hw["kim"], hw["knyq"], hw["bias"])


def _s5_kernel(u_ref, tb_ref, cg_ref, sc_ref, dcol_ref, wgt_ref, bcol_ref, o_ref, ut_scr, yt_scr):
    nk = S5_NCHUNK
    for t in range(S5_CHUNK):
        for j in range(S5_SLABS):
            blk = u_ref[j, pl.ds(t, nk, stride=S5_CHUNK), :]
            ut_scr[t, 128 * j:128 * (j + 1), :] = blk.T

    half = S5_STATE
    gb = S5_GB
    width = gb * half
    row = lax.broadcasted_iota(jnp.int32, (nk, width), 0)

    def from_prev(x, s):
        if s % 8 == 0:
            return jnp.concatenate([jnp.zeros((s, width), F32), x[:nk - s]], axis=0)
        return jnp.where(row >= s, pltpu.roll(x, s, 0), 0.0)

    def from_next(x, s):
        if s % 8 == 0:
            return jnp.concatenate([x[s:], jnp.zeros((s, width), F32)], axis=0)
        return jnp.where(row < nk - s, pltpu.roll(x, nk - s, 0), 0.0)

    def group_block(p, carry):
        rs = []
        for i in range(gb):
            r0 = pl.multiple_of((gb * p + i) * S5_GROUP, S5_GROUP)
            ucol = ut_scr[:, pl.ds(r0, S5_GROUP), :].reshape(S5_CHUNK * S5_GROUP, nk).astype(BF16)
            rs.append(_dot(tb_ref[gb * p + i], ucol))

        def states(lo):
            return jnp.concatenate([r[lo:lo + half] for r in rs], axis=0).T

        fr, br, fi, bi = states(256), states(256 + half), states(256 + 2 * half), states(256 + 3 * half)
        sc = sc_ref[p]
        for lvl in range(7):
            s = 1 << lvl
            arf, aif, arb, aib = (sc[4 * lvl + i:4 * lvl + i + 1] for i in range(4))
            pfr, pfi = from_prev(fr, s), from_prev(fi, s)
            pbr, pbi = from_next(br, s), from_next(bi, s)
            fr, fi = fr + arf * pfr - aif * pfi, fi + arf * pfi + aif * pfr
            br, bi = br + arb * pbr - aib * pbi, bi + arb * pbi + aib * pbr
        carried = [from_prev(fr, 1).T, from_next(br, 1).T, from_prev(fi, 1).T, from_next(bi, 1).T]
        for i in range(gb):
            r0 = pl.multiple_of((gb * p + i) * S5_GROUP, S5_GROUP)
            hs = jnp.concatenate([c[half * i:half * (i + 1)] for c in carried], axis=0).astype(BF16)
            z = rs[i][0:256] + _dot(cg_ref[gb * p + i], hs)
            yt_scr[:, pl.ds(r0, S5_GROUP), :] = z.reshape(S5_CHUNK, S5_GROUP, nk)
        return carry

    lax.fori_loop(0, S5_GROUPS // gb, group_block, 0)

    for t in range(S5_CHUNK):
        gl = jax.nn.gelu(yt_scr[t] + dcol_ref[...] * ut_scr[t])
        z2 = _dot(wgt_ref[...], gl.astype(BF16)) + bcol_ref[...]
        yb = gl * jax.nn.sigmoid(z2)
        for j in range(S5_SLABS):
            o_ref[j, pl.ds(t, nk, stride=S5_CHUNK), :] = yb[128 * j:128 * (j + 1), :].T


def _s5(s5_u, sw):
    t = s5_u.shape[1]
    nb = t // SEQ
    slab = pl.BlockSpec((S5_SLABS, SEQ, 128), lambda b: (0, b, 0))
    return pl.pallas_call(
        _s5_kernel,
        grid=(nb,),
        in_specs=[slab, _const_spec(sw["tb"].shape), _const_spec(sw["cg"].shape), _const_spec(sw["sc"].shape),
                  _const_spec(sw["dcol"].shape), _const_spec(sw["wgt"].shape), _const_spec(sw["bcol"].shape)],
        out_specs=slab,
        out_shape=jax.ShapeDtypeStruct((S5_SLABS, t, 128), F32),
        scratch_shapes=[pltpu.VMEM((S5_CHUNK, S5_WIDTH, S5_NCHUNK), F32),
                        pltpu.VMEM((S5_CHUNK, S5_WIDTH, S5_NCHUNK), F32)],
        compiler_params=_cparams(("arbitrary",)),
        name="s5",
    )(s5_u, sw["tb"], sw["cg"], sw["sc"], sw["dcol"], sw["wgt"], sw["bcol"])


def _ssd_kernel(xs_scr, bc_scr, dt_ref, dtt_ref, a_ref, at_ref, db_ref, dbt_ref, dexp_ref,
                ef_ref, eb_ref, tril_ref, triu_ref, o_ref,
                y_scr, carry_scr, dt_scr, a_scr, dtt_scr, att_scr, cs_scr):
    q = SSD_CHUNK
    nh = SSD_HEADS

    dtv = jax.nn.softplus(dt_ref[...] + db_ref[...])
    dt_scr[...] = dtv
    a_scr[...] = dtv * a_ref[...]
    dtvt = jax.nn.softplus(dtt_ref[...] + dbt_ref[...])
    avt = dtvt * at_ref[...]
    for c in range(SSD_NCHUNK):
        dtt_scr[c] = dtvt[:, q * c:q * (c + 1)]
        att_scr[c] = avt[:, q * c:q * (c + 1)]

    ri = lax.broadcasted_iota(jnp.int32, (q, q), 0)
    ci = lax.broadcasted_iota(jnp.int32, (q, q), 1)
    causal = ri >= ci
    anti = ci >= ri
    col_is_f = lax.broadcasted_iota(jnp.int32, (q, 2 * nh), 1) < nh
    row_is_f = lax.broadcasted_iota(jnp.int32, (2 * nh, q), 0) < nh
    lane_lo = lax.broadcasted_iota(jnp.int32, (q, 128), 1) < SSD_HEADDIM
    tril = tril_ref[...]
    triu = triu_ref[...]

    split2 = _split2

    def col_cumsum(a_c):
        parts = split2(a_c)
        lo = sum(_dot(tril, p) for p in parts)
        hi = sum(_dot(triu, p) for p in parts)
        return jnp.where(col_is_f, lo, hi)

    def row_cumsum(at_c):
        parts = split2(at_c)
        lo = sum(_dot(p, triu) for p in parts)
        hi = sum(_dot(p, tril) for p in parts)
        return jnp.where(row_is_f, lo, hi)

    def expand(v, e_ref):
        return sum(_dot(p, e_ref[...]) for p in split2(v))

    def inter_chunk(c, cs, e_ref):
        r0 = pl.multiple_of(c * q, q)
        total = jnp.where(col_is_f[0:1], cs[q - 1:q], cs[0:1])
        w_state = _dot((dt_scr[pl.ds(r0, q), :] * jnp.exp(total - cs)).astype(BF16), e_ref[...])
        w_in = _dot(jnp.exp(cs).astype(BF16), e_ref[...])
        decay = expand(jnp.broadcast_to(jnp.exp(total), (8, 2 * nh)), e_ref)[0:1]
        xdd = (xs_scr[pl.ds(r0, q), :] * w_state).astype(BF16)
        outs = []
        for g in range(SSD_GROUPS):
            bg = bc_scr[pl.ds(r0, q), SSD_STATE * g:SSD_STATE * (g + 1)]
            cg = bc_scr[pl.ds(r0, q), SSD_BC + SSD_STATE * g:SSD_BC + SSD_STATE * (g + 1)]
            lanes = slice(512 * g, 512 * (g + 1))
            prev = carry_scr[g]
            outs.append(_dot(cg, prev.astype(BF16)) * w_in[:, lanes])
            carry_scr[g] = prev * decay[:, lanes] + _dot_tn(bg, xdd[:, lanes])
        return r0, jnp.concatenate(outs, axis=1)

    def forward_chunk(c, carry):
        r0 = pl.multiple_of(c * q, q)
        cs = col_cumsum(a_scr[pl.ds(r0, q), :])
        cs_scr[pl.ds(r0, q), :] = cs
        rs = row_cumsum(att_scr[c])
        dtt_c = dtt_scr[c]
        xs_b = xs_scr[pl.ds(r0, q), :]
        pairs = []
        for g in range(SSD_GROUPS):
            bg = bc_scr[pl.ds(r0, q), SSD_STATE * g:SSD_STATE * (g + 1)]
            cg = bc_scr[pl.ds(r0, q), SSD_BC + SSD_STATE * g:SSD_BC + SSD_STATE * (g + 1)]
            scores = _dot_nt(cg, bg)
            for j in range(HEADS_PER_GROUP // 2):
                ms = []
                for h in (HEADS_PER_GROUP * g + 2 * j, HEADS_PER_GROUP * g + 2 * j + 1):
                    hb = nh + h
                    lf = jnp.where(causal, jnp.exp(cs[:, h:h + 1] - rs[h:h + 1, :]), 0.0) * dtt_c[h:h + 1, :]
                    lb = jnp.where(anti, jnp.exp(cs[:, hb:hb + 1] - rs[hb:hb + 1, :]), 0.0) * dtt_c[hb:hb + 1, :]
                    ms.append((scores * (lf + lb)).astype(BF16))
                blk = HEADS_PER_GROUP // 2 * g + j
                xp = xs_b[:, 128 * blk:128 * (blk + 1)]
                rhs = jnp.concatenate([jnp.where(lane_lo, xp, 0), jnp.where(lane_lo, 0, xp)], axis=0)
                pairs.append(_dot(jnp.concatenate(ms, axis=1), rhs))
        _, y_off = inter_chunk(c, cs, ef_ref)
        y_scr[pl.ds(r0, q), :] = jnp.concatenate(pairs, axis=1) + y_off
        return carry

    def backward_chunk(i, carry):
        c = SSD_NCHUNK - 1 - i
        r0, y_off = inter_chunk(c, cs_scr[pl.ds(pl.multiple_of(c * q, q), q), :], eb_ref)
        y_scr[pl.ds(r0, q), :] += y_off
        return carry

    carry_scr[...] = jnp.zeros_like(carry_scr)
    lax.fori_loop(0, SSD_NCHUNK, forward_chunk, 0, unroll=4)
    carry_scr[...] = jnp.zeros_like(carry_scr)
    lax.fori_loop(0, SSD_NCHUNK, backward_chunk, 0, unroll=2)

    rows = 256
    for i in range(SEQ // rows):
        sl = slice(rows * i, rows * (i + 1))
        o_ref[sl, :] = (y_scr[sl, :] + dexp_ref[...] * xs_scr[sl, :]).astype(o_ref.dtype)


def _ssd(xbc, dt, dtt, sw):
    t = xbc.shape[0]
    nb = t // SEQ
    row = lambda n: pl.BlockSpec((SEQ, n), lambda b: (b, 0))
    names = ("a", "at", "db", "dbt", "dexp", "ef", "eb", "tril", "triu")
    return pl.pallas_call(
        _ssd_kernel,
        grid=(nb,),
        in_specs=[row(SSD_WIDTH),
                  pl.BlockSpec((SEQ, 2 * SSD_BC), lambda b: (b, SSD_WIDTH // (2 * SSD_BC))),
                  row(2 * SSD_HEADS),
                  pl.BlockSpec((2 * SSD_HEADS, SEQ), lambda b: (0, b))]
                 + [_const_spec(sw[n].shape) for n in names],
        out_specs=row(SSD_WIDTH),
        out_shape=jax.ShapeDtypeStruct((t, SSD_WIDTH), BF16),
        scratch_shapes=[pltpu.VMEM((SEQ, SSD_WIDTH), F32),
                        pltpu.VMEM((SSD_GROUPS, SSD_STATE, SSD_WIDTH // SSD_GROUPS), F32),
                        pltpu.VMEM((SEQ, 2 * SSD_HEADS), F32),
                        pltpu.VMEM((SEQ, 2 * SSD_HEADS), F32),
                        pltpu.VMEM((SSD_NCHUNK, 2 * SSD_HEADS, SSD_CHUNK), F32),
                        pltpu.VMEM((SSD_NCHUNK, 2 * SSD_HEADS, SSD_CHUNK), F32),
                        pltpu.VMEM((SEQ, 2 * SSD_HEADS), F32)],
        compiler_params=_cparams(("arbitrary",)),
        name="ssd",
    )(xbc, xbc, dt, dtt, *[sw[n] for n in names])


def _merge_kernel(x_ref, ya_ref, yb_ref, yc_ref, z_ref, g_ref, pa_ref, pb_ref, pc_ref, wo_ref, nc_ref, nw_ref,
                  xo_ref, h_ref):
    gate = g_ref[...].astype(F32)
    yb = jnp.concatenate([yb_ref[j] for j in range(S5_SLABS)], axis=1).astype(BF16)
    yc = _rms(yc_ref[...].astype(F32) * z_ref[...].astype(F32), nc_ref[...]).astype(BF16)
    merged = (gate[:, 0:D_MODEL] * _dot(ya_ref[...], pa_ref[...])
              + gate[:, D_MODEL:2 * D_MODEL] * _dot(yb, pb_ref[...])
              + gate[:, 2 * D_MODEL:] * _dot(yc, pc_ref[...]))
    xn = x_ref[...] + _dot(merged.astype(BF16), wo_ref[...])
    xo_ref[...] = xn
    h_ref[...] = _rms(xn, nw_ref[...]).astype(BF16)


def _merge(x2, ya, yb, yc, z, gates, mw):
    t = x2.shape[0]
    tm = PROJ_TM
    row = lambda n: pl.BlockSpec((tm, n), lambda i: (i, 0))
    return pl.pallas_call(
        _merge_kernel,
        grid=(t // tm,),
        in_specs=[row(D_MODEL), row(HY_WIDTH), pl.BlockSpec((S5_SLABS, tm, 128), lambda i: (0, i, 0)),
                  row(SSD_WIDTH), row(SSD_WIDTH), row(3 * D_MODEL),
                  _const_spec(mw["pa"].shape), _const_spec(mw["pb"].shape), _const_spec(mw["pc"].shape),
                  _const_spec(mw["wo"].shape), _const_spec((1, SSD_WIDTH)), _const_spec((1, D_MODEL))],
        out_specs=[row(D_MODEL), row(D_MODEL)],
        out_shape=[jax.ShapeDtypeStruct((t, D_MODEL), F32), jax.ShapeDtypeStruct((t, D_MODEL), BF16)],
        compiler_params=_cparams(("arbitrary",)),
        name="merge",
    )(x2, ya, yb, yc, z, gates, mw["pa"], mw["pb"], mw["pc"], mw["wo"], mw["nc"], mw["nw"])


def _ffn_kernel(h_ref, hp_ref, hn_ref, x_ref, wu_ref, cw_ref, cb_ref, wd_ref, nf_ref, o_ref, stage_a, stage_b,
                *, final_norm):
    tm = h_ref.shape[0]
    tiles_per_seq = SEQ // tm
    pos = pl.program_id(0) % tiles_per_seq
    prev = jnp.where(pos == 0, jnp.zeros_like(hp_ref[...]), hp_ref[...])
    nxt = jnp.where(pos == tiles_per_seq - 1, jnp.zeros_like(hn_ref[...]), hn_ref[...])
    hext = jnp.concatenate([prev, h_ref[...], nxt], axis=0)

    def conv(cols, stage):
        stage[...] = _dot(hext, wu_ref[:, cols])
        acc = None
        for k in range(3):
            term = stage[pl.ds(ROW_HALO + k - 1, tm), :].reshape(tm // 8, 8, FFN_TILE) * cw_ref[k, :, cols]
            acc = term if acc is None else acc + term
        return (acc + cb_ref[:, cols]).reshape(tm, FFN_TILE)

    acts = []
    for j in range(FFN_NTILE):
        gate = conv(slice(FFN_TILE * j, FFN_TILE * (j + 1)), stage_a)
        val = conv(slice(FFN_HIDDEN + FFN_TILE * j, FFN_HIDDEN + FFN_TILE * (j + 1)), stage_b)
        acts.append((gate * jax.nn.sigmoid(gate) * val).astype(BF16))
    y = x_ref[...] + _dot(jnp.concatenate(acts, axis=1), wd_ref[...])
    o_ref[...] = _rms(y, nf_ref[...]) if final_norm else y


def _ffn(h2, x2, fw, nf, final_norm):
    t = x2.shape[0]
    tm = PROJ_TM
    hb = tm // ROW_HALO
    last = t // ROW_HALO - 1
    row = lambda n: pl.BlockSpec((tm, n), lambda i: (i, 0))
    return pl.pallas_call(
        functools.partial(_ffn_kernel, final_norm=final_norm),
        grid=(t // tm,),
        in_specs=[row(D_MODEL),
                  pl.BlockSpec((ROW_HALO, D_MODEL), lambda i: (jnp.maximum(i * hb - 1, 0), 0)),
                  pl.BlockSpec((ROW_HALO, D_MODEL), lambda i: (jnp.minimum((i + 1) * hb, last), 0)),
                  row(D_MODEL),
                  _const_spec(fw["up"].shape), _const_spec(fw["cw"].shape), _const_spec(fw["cb"].shape),
                  _const_spec(fw["down"].shape), _const_spec((1, D_MODEL))],
        out_specs=row(D_MODEL),
        out_shape=jax.ShapeDtypeStruct((t, D_MODEL), F32),
        scratch_shapes=[pltpu.VMEM((tm + 2 * ROW_HALO, FFN_TILE), F32)] * 2,
        compiler_params=_cparams(("arbitrary",)),
        name="conv_ffn",
    )(h2, h2, h2, x2, fw["up"], fw["cw"], fw["cb"], fw["down"], nf)


_HI = lax.Precision.HIGHEST


def _rep8(v):
    return jnp.broadcast_to(v[..., None, :], v.shape[:-1] + (8, v.shape[-1]))


def _prep_proj(w_in, hy_short_w, hy_short_b, ssd_conv_w, ssd_conv_b):
    o = [0, 3 * HY_WIDTH, 3 * HY_WIDTH + S5_WIDTH]
    o.append(o[-1] + SSD_WIDTH)
    o.append(o[-1] + SSD_XBC)
    o.append(o[-1] + 2 * SSD_HEADS)
    cut = lambda a, b: w_in[:, a:b].astype(BF16)
    return {"hy": cut(o[0], o[1]), "s5": cut(o[1], o[2]), "z": cut(o[2], o[3]), "xbc": cut(o[3], o[4]),
            "dt": cut(o[4], o[5]), "dtt": cut(o[4], o[5]).T, "gate": cut(o[5], w_in.shape[1]),
            "hcw": _rep8(hy_short_w), "hcb": _rep8(hy_short_b), "scw": _rep8(ssd_conv_w), "scb": _rep8(ssd_conv_b)}


def _hyena_taps(w1, b1, freq, w2, b2, w3, decay):
    pos = jnp.arange(SEQ, dtype=F32)
    tt = (pos / max(SEQ - 1, 1))[:, None]
    bands = jnp.linspace(1e-4, HY_BANDS - 1, HY_BANDS, dtype=F32)
    ang = (2.0 * math.pi / SEQ) * pos[:, None] * bands[None, :]
    zz = jnp.concatenate([tt, jnp.cos(ang), -jnp.sin(ang)], axis=-1)
    h = jnp.sin(freq[0] * (jnp.dot(zz, w1, precision=_HI) + b1))
    h = jnp.sin(freq[1] * (jnp.dot(h, w2, precision=_HI) + b2))
    return jnp.dot(h, w3, precision=_HI) * jnp.exp(-tt * jnp.abs(decay))


def _spectrum_kernel(hf_ref, hb_ref, fc_ref, fs_ref, kre_ref, kim_ref, knyq_ref):
    row = lax.broadcasted_iota(jnp.int32, (SEQ, HY_TILE), 0)
    hf = hf_ref[...]
    hb = jnp.where(row == 0, 0.0, hb_ref[...])
    inv = 1.0 / (NFFT * (jnp.sum(jnp.abs(hf), axis=0, keepdims=True) + jnp.sum(jnp.abs(hb), axis=0, keepdims=True)))
    even = hf + hb
    odd = hb - hf
    dft = lambda m_ref, v: sum(_dot(m_ref[...], p) for p in _split2(v))
    freq = lax.broadcasted_iota(jnp.int32, (HY_FTILE, HY_TILE), 0) + pl.program_id(1) * HY_FTILE
    scale = jnp.where(freq == 0, 1.0, 2.0) * inv
    kre_ref[...] = dft(fc_ref, even) * scale
    kim_ref[...] = dft(fs_ref, odd) * scale
    alt = jnp.where((row & 1) == 0, 1.0, -1.0)
    knyq_ref[...] = jnp.sum(even * alt, axis=0, keepdims=True) * inv


def _hyena_spectrum(taps, fc, fs):
    nj = HY_WIDTH // HY_TILE
    col = lambda r, off: pl.BlockSpec((r, HY_TILE), lambda j, f, off=off: (0, off + j))
    frow = pl.BlockSpec((HY_FTILE, SEQ), lambda j, f: (f, 0))
    fout = pl.BlockSpec((HY_FTILE, HY_TILE), lambda j, f: (f, j))
    return pl.pallas_call(
        _spectrum_kernel,
        grid=(nj, SEQ // HY_FTILE),
        in_specs=[col(SEQ, 0), col(SEQ, nj), frow, frow],
        out_specs=[fout, fout, col(1, 0)],
        out_shape=[jax.ShapeDtypeStruct((SEQ, HY_WIDTH), F32), jax.ShapeDtypeStruct((SEQ, HY_WIDTH), F32),
                   jax.ShapeDtypeStruct((1, HY_WIDTH), F32)],
        compiler_params=_cparams(("arbitrary", "arbitrary")),
        name="hyena_spectrum",
    )(taps, taps, fc, fs)


def _prep_s5(a_re, a_im, log_dt, b_re, b_im, c_re, c_im, d, w_glu, b_glu):
    q, gs, ns, ng = S5_CHUNK, S5_GROUP, S5_STATE, S5_GROUPS
    dt = jnp.exp(log_dt)[:, :, None]
    mag = jnp.exp(a_re * dt)
    ab_r, ab_i = mag * jnp.cos(a_im * dt), mag * jnp.sin(a_im * dt)
    den = a_re * a_re + a_im * a_im
    f_r = ((ab_r - 1.0) * a_re + ab_i * a_im) / den
    f_i = (ab_i * a_re - (ab_r - 1.0) * a_im) / den
    fb_r = f_r[..., None] * b_re - f_i[..., None] * b_im
    fb_i = f_r[..., None] * b_im + f_i[..., None] * b_re

    def power(j):
        m = jnp.exp(j * a_re * dt)
        return m * jnp.cos(j * a_im * dt), m * jnp.sin(j * a_im * dt)

    p_r, p_i = zip(*[power(float(j)) for j in range(q + 1)])
    p_r, p_i = jnp.stack(p_r, 2), jnp.stack(p_i, 2)

    cp_r = c_re[:, :, None] * p_r[:, :, :q, None] - c_im[:, :, None] * p_i[:, :, :q, None]
    cp_i = c_re[:, :, None] * p_i[:, :, :q, None] + c_im[:, :, None] * p_r[:, :, :q, None]
    kd = (jnp.einsum("zgdcn,zgne->zgdce", cp_r, fb_r, precision=_HI)
          - jnp.einsum("zgdcn,zgne->zgdce", cp_i, fb_i, precision=_HI))
    lag = jnp.arange(q)[:, None] - jnp.arange(q)[None, :]
    dd = jnp.arange(q)[:, None, None]
    sel_f = (lag[None] == dd).astype(F32)
    sel_b = (-lag[None] == dd).astype(F32)
    toep = (jnp.einsum("dtu,gdce->gtcue", sel_f, kd[0], precision=_HI)
            + jnp.einsum("dtu,gdce->gtcue", sel_b, kd[1], precision=_HI)).reshape(ng, q * gs, q * gs)

    def inject(z, pw):
        pr, pi = p_r[z][:, pw], p_i[z][:, pw]
        re = pr[..., None] * fb_r[z][:, None] - pi[..., None] * fb_i[z][:, None]
        im = pr[..., None] * fb_i[z][:, None] + pi[..., None] * fb_r[z][:, None]
        to_rows = lambda v: v.transpose(0, 2, 1, 3).reshape(ng, ns, q * gs)
        return to_rows(re), to_rows(im)

    bf_r, bf_i = inject(0, jnp.arange(q - 1, -1, -1))
    bb_r, bb_i = inject(1, jnp.arange(q))
    tbm = jnp.concatenate([toep, bf_r, bb_r, bf_i, bb_i], axis=1).astype(BF16)

    def readout(z, pw):
        pr, pi = p_r[z][:, pw], p_i[z][:, pw]
        re = c_re[z][:, None] * pr[:, :, None] - c_im[z][:, None] * pi[:, :, None]
        im = c_re[z][:, None] * pi[:, :, None] + c_im[z][:, None] * pr[:, :, None]
        return re.reshape(ng, q * gs, ns), -im.reshape(ng, q * gs, ns)

    cf_r, cf_i = readout(0, jnp.arange(1, q + 1))
    cb_r, cb_i = readout(1, jnp.arange(q, 0, -1))
    cgm = jnp.concatenate([cf_r, cb_r, cf_i, cb_i], axis=2).astype(BF16)

    rows = []
    for lvl in range(7):
        pr, pi = power(float(q << lvl))
        rows += [v.reshape(ng // S5_GB, S5_GB * ns) for v in (pr[0], pi[0], pr[1], pi[1])]
    rows += [jnp.zeros_like(rows[0])] * 4
    sc = jnp.stack(rows, axis=1)
    ones = jnp.ones((1, S5_NCHUNK), F32)
    return {"tb": tbm, "cg": cgm, "sc": sc, "dcol": d[:, None] * ones, "wgt": w_glu.T.astype(BF16),
            "bcol": b_glu[:, None] * ones}


def _prep_ssd(a_log, dt_bias, d):
    nh = SSD_HEADS
    a = (-jnp.exp(a_log)).reshape(1, 2 * nh)
    db = dt_bias.reshape(1, 2 * nh)
    head_of_lane = jnp.arange(SSD_WIDTH) // SSD_HEADDIM
    onehot = (jnp.arange(nh)[:, None] == head_of_lane[None, :]).astype(BF16)
    zero = jnp.zeros_like(onehot)
    i = jnp.arange(SSD_CHUNK)
    return {"a": a, "at": a.T, "db": db, "dbt": db.T,
            "dexp": jnp.repeat(d, SSD_HEADDIM).reshape(1, SSD_WIDTH),
            "ef": jnp.concatenate([onehot, zero], 0), "eb": jnp.concatenate([zero, onehot], 0),
            "tril": (i[:, None] >= i[None, :]).astype(BF16), "triu": (i[:, None] <= i[None, :]).astype(BF16)}


def kernel(x, norm_mix, w_in, hy_short_w, hy_short_b, hy_w1, hy_b1, hy_freq, hy_w2, hy_b2, hy_w3, hy_decay, hy_bias, s5_a_re, s5_a_im, s5_log_dt, s5_b_re, s5_b_im, s5_c_re, s5_c_im, s5_d, s5_w_glu, s5_b_glu, ssd_conv_w, ssd_conv_b, ssd_a_log, ssd_dt_bias, ssd_d, ssd_norm, p_a, p_b, p_c, w_out, norm_ffn, ffn_up, ffn_conv_w, ffn_conv_b, ffn_down, norm_final):
    bsz, seq, dm = x.shape
    assert (seq, dm) == (SEQ, D_MODEL)
    x2 = x.reshape(bsz * seq, dm)
    fc, fs = _dft_matrices()
    nf = norm_final.reshape(1, D_MODEL)
    for l in range(DEPTH):
        pw = _prep_proj(w_in[l], hy_short_w[l], hy_short_b[l], ssd_conv_w[l], ssd_conv_b[l])
        kre, kim, knyq = _hyena_spectrum(
            _hyena_taps(hy_w1[l], hy_b1[l], hy_freq[l], hy_w2[l], hy_b2[l], hy_w3[l], hy_decay[l]), fc, fs)
        hw = {"kre": kre, "kim": kim, "knyq": knyq, "bias": hy_bias[l].reshape(1, HY_WIDTH)}
        s5w = _prep_s5(s5_a_re[l], s5_a_im[l], s5_log_dt[l], s5_b_re[l], s5_b_im[l], s5_c_re[l], s5_c_im[l],
                       s5_d[l], s5_w_glu[l], s5_b_glu[l])
        sdw = _prep_ssd(ssd_a_log[l], ssd_dt_bias[l], ssd_d[l])
        mw = {"pa": p_a[l].astype(BF16), "pb": p_b[l].astype(BF16), "pc": p_c[l].astype(BF16),
              "wo": w_out[l].astype(BF16), "nc": ssd_norm[l].reshape(1, SSD_WIDTH),
              "nw": norm_ffn[l].reshape(1, D_MODEL)}
        fw = {"up": ffn_up[l].astype(BF16), "cw": _rep8(ffn_conv_w[l]), "cb": _rep8(ffn_conv_b[l]),
              "down": ffn_down[l].astype(BF16)}

        hy_u, s5_u, z, xbc, dt, dtt, gates = _proj(x2, norm_mix[l].reshape(1, D_MODEL), pw)
        ya = _hyena(hy_u, hw, fc, fs)
        yb = _s5(s5_u, s5w)
        yc = _ssd(xbc, dt, dtt, sdw)
        x_mid, h2 = _merge(x2, ya, yb, yc, z, gates, mw)
        x2 = _ffn(h2, x_mid, fw, nf, final_norm=(l == DEPTH - 1))
    return x2.reshape(bsz, seq, dm)
```
